```python
import jax, jax.numpy as jnp
from jax import lax

D_MODEL = 1024
BATCH = 4
SEQ = 4096
DEPTH = 1

N_ATTN_HEADS = 8
HEAD_DIM = 64
ATTN_WIDTH = N_ATTN_HEADS * HEAD_DIM
N_CONV_GROUPS = 8
CONV_GROUP_DIM = 64
CONV_WIDTH = N_CONV_GROUPS * CONV_GROUP_DIM
CONV_KSIZE = 3
D_FF = 2816
Q_BLOCK = 128
N_SUBLAYERS = 3
N_MOD = 3
EPS = 1e-6
FFN_RES_WEIGHT = 0.5
MIX_IN_WIDTH = 3 * CONV_WIDTH + 3 * ATTN_WIDTH + 2 * D_MODEL

kernel_name = "hybrid_shortconv_stickbreaking_macaron_block"


def rmsnorm(x, g):
    xf = x.astype(jnp.float32)
    inv = lax.rsqrt(jnp.mean(xf * xf, axis=-1, keepdims=True) + EPS)
    return (xf * inv).astype(x.dtype) * g


def modulate(x, shift, scale):
    return x * (1 + scale[:, None, :]) + shift[:, None, :]


def swiglu(x, w_gu, w_down):
    g, u = jnp.split(x @ w_gu, 2, axis=-1)
    return (jax.nn.silu(g) * u) @ w_down


def short_conv(b_gate, c_gate, xin, conv_w):
    v = c_gate * xin
    S = v.shape[1]
    vp = jnp.pad(v, ((0, 0), (CONV_KSIZE - 1, 0), (0, 0)))
    y = conv_w[0] * vp[:, 0:S, :]
    for k in range(1, CONV_KSIZE):
        y = y + conv_w[k] * vp[:, k:k + S, :]
    return b_gate * y


def stick_breaking_attention(q, k, v):
    S = q.shape[2]
    scale = HEAD_DIM ** -0.5
    qf = q.astype(jnp.float32)
    kf = k.astype(jnp.float32)
    vf = v.astype(jnp.float32)
    outs = []
    for i in range(S // Q_BLOCK):
        start = i * Q_BLOCK
        end = start + Q_BLOCK
        q_blk = qf[:, :, start:end, :]
        k_c = kf[:, :, :end, :]
        v_c = vf[:, :, :end, :]
        z = jnp.einsum('bhqd,bhkd->bhqk', q_blk, k_c) * scale
        t_pos = start + jnp.arange(Q_BLOCK)
        s_pos = jnp.arange(end)
        valid = s_pos[None, :] < t_pos[:, None]
        sp = jnp.where(valid, jax.nn.softplus(z), 0.0)
        rem = lax.cumsum(sp, axis=3, reverse=True) - sp
        log_a = jax.nn.log_sigmoid(z) - rem
        a = jnp.where(valid, jnp.exp(log_a), 0.0)
        outs.append(jnp.einsum('bhqk,bhkd->bhqd', a, v_c))
    return jnp.concatenate(outs, axis=2).astype(q.dtype)


def mixer(u, w_mix_in, b_merge, conv_w, w_conv_out, w_attn_out, w_out):
    B, S, _ = u.shape
    proj = u @ w_mix_in
    idx = [CONV_WIDTH, 2 * CONV_WIDTH, 3 * CONV_WIDTH,
           3 * CONV_WIDTH + ATTN_WIDTH, 3 * CONV_WIDTH + 2 * ATTN_WIDTH,
           3 * CONV_WIDTH + 3 * ATTN_WIDTH, 3 * CONV_WIDTH + 3 * ATTN_WIDTH + D_MODEL]
    cb, cc, cx, q, k, v, ga, gb = jnp.split(proj, idx, axis=-1)
    ya = short_conv(cb, cc, cx, conv_w) @ w_conv_out
    def heads(t):
        return t.reshape(B, S, N_ATTN_HEADS, HEAD_DIM).transpose(0, 2, 1, 3)
    o = stick_breaking_attention(heads(q), heads(k), heads(v))
    o = o.transpose(0, 2, 1, 3).reshape(B, S, ATTN_WIDTH)
    yb = o @ w_attn_out
    merged = jax.nn.sigmoid(ga + b_merge[0]) * ya + jax.nn.sigmoid(gb + b_merge[1]) * yb
    return merged @ w_out


def setup_inputs(seed: int = 0) -> dict:
    key = jax.random.key(seed)
    ks = jax.random.split(key, 24)
    f32 = jnp.float32
    L, D = DEPTH, D_MODEL

    def nrm(k, shape, s):
        return jax.random.normal(k, shape, f32) * s

    return {
        "x": nrm(ks[0], (BATCH, SEQ, D), 1.0),
        "c": nrm(ks[1], (BATCH, D), 1.0),
        "w_ada": nrm(ks[2], (L, D, N_SUBLAYERS * N_MOD * D), 0.5 * D ** -0.5),
        "b_ada": nrm(ks[3], (L, N_SUBLAYERS * N_MOD * D), 0.02),
        "norm1_g": 1.0 + nrm(ks[4], (L, D), 0.02),
        "ffn1_w_gu": nrm(ks[5], (L, D, 2 * D_FF), D ** -0.5),
        "ffn1_w_down": nrm(ks[6], (L, D_FF, D), D_FF ** -0.5),
        "norm2_g": 1.0 + nrm(ks[7], (L, D), 0.02),
        "w_mix_in": nrm(ks[8], (L, D, MIX_IN_WIDTH), D ** -0.5),
        "b_merge": nrm(ks[9], (L, 2, D), 0.02),
        "conv_w": nrm(ks[10], (L, CONV_KSIZE, CONV_WIDTH), CONV_KSIZE ** -0.5),
        "w_conv_out": nrm(ks[11], (L, CONV_WIDTH, D), CONV_WIDTH ** -0.5),
        "w_attn_out": nrm(ks[12], (L, ATTN_WIDTH, D), ATTN_WIDTH ** -0.5),
        "w_out": nrm(ks[13], (L, D, D), D ** -0.5),
        "norm3_g": 1.0 + nrm(ks[14], (L, D), 0.02),
        "ffn2_w_gu": nrm(ks[15], (L, D, 2 * D_FF), D ** -0.5),
        "ffn2_w_down": nrm(ks[16], (L, D_FF, D), D_FF ** -0.5),
        "final_g": 1.0 + nrm(ks[17], (D,), 0.02),
    }


def reference(x, c, w_ada, b_ada, norm1_g, ffn1_w_gu, ffn1_w_down, norm2_g,
              w_mix_in, b_merge, conv_w, w_conv_out, w_attn_out, w_out,
              norm3_g, ffn2_w_gu, ffn2_w_down, final_g):
    B = x.shape[0]
    c_act = jax.nn.silu(c)
    h = x
    for l in range(DEPTH):
        mod = (c_act @ w_ada[l] + b_ada[l]).reshape(B, N_SUBLAYERS, N_MOD, D_MODEL)
        u = modulate(rmsnorm(h, norm1_g[l]), mod[:, 0, 0], mod[:, 0, 1])
        h = h + FFN_RES_WEIGHT * mod[:, 0, 2][:, None, :] * swiglu(u, ffn1_w_gu[l], ffn1_w_down[l])
        u = modulate(rmsnorm(h, norm2_g[l]), mod[:, 1, 0], mod[:, 1, 1])
        y = mixer(u, w_mix_in[l], b_merge[l], conv_w[l], w_conv_out[l], w_attn_out[l], w_out[l])
        h = h + mod[:, 1, 2][:, None, :] * y
        u = modulate(rmsnorm(h, norm3_g[l]), mod[:, 2, 0], mod[:, 2, 1])
        h = h + FFN_RES_WEIGHT * mod[:, 2, 2][:, None, :] * swiglu(u, ffn2_w_gu[l], ffn2_w_down[l])
    return rmsnorm(h, final_g)
```

```python
import functools

import jax
import jax.numpy as jnp
from jax import lax
from jax.experimental import pallas as pl
from jax.experimental.pallas import tpu as pltpu

D_MODEL = 1024
N_HEADS = 8
HEAD_DIM = 64
ATTN_WIDTH = N_HEADS * HEAD_DIM
CONV_WIDTH = 512
CONV_KSIZE = 3
D_FF = 2816
N_SUBLAYERS = 3
N_MOD = 3
EPS = 1e-6
FFN_RES_WEIGHT = 0.5
MIX_IN_WIDTH = 3 * CONV_WIDTH + 3 * ATTN_WIDTH + 2 * D_MODEL

LANES = 128
SUBLANES = 8
MXU_DIM = 256
VMEM_LIMIT = 56 * 1024 * 1024

ROW_TILE = 512
FF_CHUNK = MXU_DIM
N_FF_CHUNKS = D_FF // FF_CHUNK
ATTN_BLOCK = 128
SKIP_LOG = 105.0

assert D_FF % FF_CHUNK == 0

_bf16 = jnp.bfloat16
_f32 = jnp.float32


def _dot(a, b):
    return jnp.dot(a, b, preferred_element_type=_f32)


def _resident(shape):
    zeros = (0,) * len(shape)
    return pl.BlockSpec(shape, lambda *_: zeros, pipeline_mode=pl.Buffered(1))


def _norm_mod(x, g, shift, scale):
    inv = lax.rsqrt(jnp.mean(x * x, axis=-1, keepdims=True) + EPS)
    return (x * inv) * g * (1.0 + scale) + shift


def _swiglu_into(acc_ref, u_bf16, wgu_ref, wd_ref):
    acc_ref[...] = jnp.zeros_like(acc_ref)

    def body(c, carry):
        gu = _dot(u_bf16, wgu_ref[c])
        g = gu[:, :FF_CHUNK]
        up = gu[:, FF_CHUNK:]
        act = (g * jax.nn.sigmoid(g) * up).astype(_bf16)
        acc_ref[...] += _dot(act, wd_ref[c])
        return carry

    lax.fori_loop(0, N_FF_CHUNKS, body, 0)


def _adaln_kernel(c_ref, w_ref, b_ref, o_ref):
    c = c_ref[...]
    c_act = c * jax.nn.sigmoid(c)
    o_ref[...] = jnp.dot(c_act, w_ref[...], preferred_element_type=_f32,
                         precision=lax.Precision.HIGHEST) + b_ref[...]


def _adaln(c, w_ada, b_ada):
    batch, d = c.shape
    n = w_ada.shape[1]
    bn = 1024
    return pl.pallas_call(
        _adaln_kernel,
        grid=(n // bn,),
        in_specs=[
            pl.BlockSpec((batch, d), lambda j: (0, 0)),
            pl.BlockSpec((d, bn), lambda j: (0, j)),
            pl.BlockSpec((1, bn), lambda j: (0, j)),
        ],
        out_specs=pl.BlockSpec((batch, bn), lambda j: (0, j)),
        out_shape=jax.ShapeDtypeStruct((batch, n), _f32),
        compiler_params=pltpu.CompilerParams(
            dimension_semantics=("arbitrary",), vmem_limit_bytes=VMEM_LIMIT),
        name="adaln",
    )(c, w_ada, b_ada.reshape(1, n))


def _ffn1_kernel(h_ref, mod_ref, g_ref, wgu_ref, wd_ref, o_ref, acc_ref):
    x = h_ref[...]
    shift = mod_ref[0, 0:1, :]
    scale = mod_ref[0, 1:2, :]
    gate = mod_ref[0, 2:3, :]
    u = _norm_mod(x, g_ref[...], shift, scale).astype(_bf16)
    _swiglu_into(acc_ref, u, wgu_ref, wd_ref)
    o_ref[...] = x + (FFN_RES_WEIGHT * gate) * acc_ref[...]


def _ffn1(h, mod, norm_g, wgu, wd, seq):
    t, d = h.shape
    tiles_per_seq = seq // ROW_TILE
    return pl.pallas_call(
        _ffn1_kernel,
        grid=(t // ROW_TILE,),
        in_specs=[
            pl.BlockSpec((ROW_TILE, d), lambda i: (i, 0)),
            pl.BlockSpec((1, N_SUBLAYERS * N_MOD, d), lambda i: (i // tiles_per_seq, 0, 0)),
            _resident((1, d)),
            _resident(wgu.shape),
            _resident(wd.shape),
        ],
        out_specs=pl.BlockSpec((ROW_TILE, d), lambda i: (i, 0)),
        out_shape=jax.ShapeDtypeStruct((t, d), _f32),
        scratch_shapes=[pltpu.VMEM((ROW_TILE, d), _f32)],
        compiler_params=pltpu.CompilerParams(
            dimension_semantics=("arbitrary",), vmem_limit_bytes=VMEM_LIMIT),
        name="ffn1",
    )(h, mod, norm_g, wgu, wd)


def _mixproj_kernel(tiles_per_seq, h_ref, mod_ref, g_ref, w_ref, bm_ref, cw_ref, wco_ref,
                    q_ref, k_ref, v_ref, p1_ref, sb_ref, vbuf_ref):
    i = pl.program_id(0)
    rows = h_ref.shape[0]
    shift = mod_ref[0, 3:4, :]
    scale = mod_ref[0, 4:5, :]
    u = _norm_mod(h_ref[...], g_ref[...], shift, scale).astype(_bf16)

    @pl.when(i % tiles_per_seq == 0)
    def _():
        vbuf_ref[0:SUBLANES, :] = jnp.zeros((SUBLANES, CONV_WIDTH), _f32)

    pc = _dot(u, w_ref[:, 0:3 * CONV_WIDTH])
    cb = pc[:, 0:CONV_WIDTH]
    cv = pc[:, CONV_WIDTH:2 * CONV_WIDTH] * pc[:, 2 * CONV_WIDTH:3 * CONV_WIDTH]
    vbuf_ref[SUBLANES:SUBLANES + rows, :] = cv
    y = (cw_ref[0:1, :] * vbuf_ref[SUBLANES - 2:SUBLANES - 2 + rows, :]
         + cw_ref[1:2, :] * vbuf_ref[SUBLANES - 1:SUBLANES - 1 + rows, :]
         + cw_ref[2:3, :] * cv)
    vbuf_ref[0:SUBLANES, :] = cv[rows - SUBLANES:, :]
    ya = _dot((cb * y).astype(_bf16), wco_ref[...])

    qkv0 = 3 * CONV_WIDTH
    pq = _dot(u, w_ref[:, qkv0:qkv0 + 3 * ATTN_WIDTH])
    q_ref[...] = (pq[:, 0:ATTN_WIDTH] * (HEAD_DIM ** -0.5)).astype(_bf16)
    k_ref[...] = pq[:, ATTN_WIDTH:2 * ATTN_WIDTH].astype(_bf16)
    v_ref[...] = pq[:, 2 * ATTN_WIDTH:3 * ATTN_WIDTH].astype(_bf16)

    g0 = qkv0 + 3 * ATTN_WIDTH
    pg = _dot(u, w_ref[:, g0:g0 + 2 * D_MODEL])
    sa = jax.nn.sigmoid(pg[:, 0:D_MODEL] + bm_ref[0:1, :])
    sb = jax.nn.sigmoid(pg[:, D_MODEL:2 * D_MODEL] + bm_ref[1:2, :])
    p1_ref[...] = (sa * ya).astype(_bf16)
    sb_ref[...] = sb.astype(_bf16)


def _mixproj(h, mod, norm_g, w_mix, b_merge, conv_w, w_conv_out, seq):
    t, d = h.shape
    tiles_per_seq = seq // ROW_TILE
    row = lambda width: pl.BlockSpec((ROW_TILE, width), lambda i: (i, 0))
    return pl.pallas_call(
        functools.partial(_mixproj_kernel, tiles_per_seq),
        grid=(t // ROW_TILE,),
        in_specs=[
            row(d),
            pl.BlockSpec((1, N_SUBLAYERS * N_MOD, d), lambda i: (i // tiles_per_seq, 0, 0)),
            _resident((1, d)),
            _resident(w_mix.shape),
            _resident(b_merge.shape),
            _resident(conv_w.shape),
            _resident(w_conv_out.shape),
        ],
        out_specs=[row(ATTN_WIDTH), row(ATTN_WIDTH), row(ATTN_WIDTH), row(d), row(d)],
        out_shape=[jax.ShapeDtypeStruct((t, ATTN_WIDTH), _bf16)] * 3
        + [jax.ShapeDtypeStruct((t, d), _bf16)] * 2,
        scratch_shapes=[pltpu.VMEM((SUBLANES + ROW_TILE, CONV_WIDTH), _f32)],
        compiler_params=pltpu.CompilerParams(
            dimension_semantics=("arbitrary",), vmem_limit_bytes=VMEM_LIMIT),
        name="mixproj",
    )(h, mod, norm_g, w_mix, b_merge, conv_w, w_conv_out)


def _attn_kernel(q_ref, k_ref, v_ref, tri_ref, o_ref, acc_ref, rem_ref):
    iq = pl.program_id(2)
    bq = ATTN_BLOCK
    lane = lax.broadcasted_iota(jnp.int32, (bq, LANES), 1)
    first_head = lane < HEAD_DIM
    q = q_ref[0]
    zero = jnp.zeros_like(q)
    q2 = jnp.concatenate([jnp.where(first_head, q, zero), jnp.where(first_head, zero, q)], axis=0)

    row = lax.broadcasted_iota(jnp.int32, (2 * bq, bq), 0)
    col = lax.broadcasted_iota(jnp.int32, (2 * bq, bq), 1)
    strictly_causal = col < jnp.where(row >= bq, row - bq, row)

    acc_ref[...] = jnp.zeros_like(acc_ref)
    rem_ref[...] = jnp.zeros_like(rem_ref)

    def visit(j, diagonal):
        start = pl.multiple_of(j * bq, bq)
        kb = k_ref[0, pl.ds(start, bq), :]
        vb = v_ref[0, pl.ds(start, bq), :]
        z = lax.dot_general(q2, kb, (((1,), (1,)), ((), ())), preferred_element_type=_f32)
        sp = jnp.maximum(z, 0.0) + jnp.log(1.0 + jnp.exp(-jnp.abs(z)))
        if diagonal:
            sp = jnp.where(strictly_causal, sp, 0.0)
        hi = sp.astype(_bf16)
        lo = (sp - hi.astype(_f32)).astype(_bf16)
        cs = _dot(jnp.concatenate([hi, lo], axis=1), tri_ref[...])
        incl = cs[:, :bq]
        total = cs[:, bq:]
        a = jnp.exp(z - incl - rem_ref[...])
        if diagonal:
            a = jnp.where(strictly_causal, a, 0.0)
        a = a.astype(_bf16)
        a_pair = jnp.concatenate([a[:bq], a[bq:]], axis=1)
        zv = jnp.zeros_like(vb)
        v_pair = jnp.concatenate([jnp.where(first_head, vb, zv), jnp.where(first_head, zv, vb)], axis=0)
        acc_ref[...] += _dot(a_pair, v_pair)
        rem_ref[...] += total

    visit(iq, True)

    def cond(state):
        j, live = state
        return jnp.logical_and(j >= 0, live)

    def body(state):
        j, _ = state
        visit(j, False)
        return j - 1, jnp.min(rem_ref[...]) < SKIP_LOG

    lax.while_loop(cond, body, (iq - 1, True))
    o_ref[0] = acc_ref[...].astype(o_ref.dtype)


def _tri_matrix():
    bq = ATTN_BLOCK
    j = lax.broadcasted_iota(jnp.int32, (bq, bq), 0)
    s = lax.broadcasted_iota(jnp.int32, (bq, bq), 1)
    tri = (j >= s).astype(_bf16)
    half = jnp.concatenate([tri, jnp.ones((bq, bq), _bf16)], axis=1)
    return jnp.concatenate([half, half], axis=0)


def _attention(q, k, v):
    batch, seq, width = q.shape
    pairs = width // LANES
    bq = ATTN_BLOCK
    blk = pl.BlockSpec((1, bq, LANES), lambda b, p, i: (b, i, p))
    whole = pl.BlockSpec((1, seq, LANES), lambda b, p, i: (b, 0, p))
    return pl.pallas_call(
        _attn_kernel,
        grid=(batch, pairs, seq // bq),
        in_specs=[blk, whole, whole, _resident((2 * bq, 2 * bq))],
        out_specs=blk,
        out_shape=jax.ShapeDtypeStruct((batch, seq, width), _bf16),
        scratch_shapes=[pltpu.VMEM((bq, LANES), _f32), pltpu.VMEM((2 * bq, bq), _f32)],
        compiler_params=pltpu.CompilerParams(
            dimension_semantics=("arbitrary", "arbitrary", "arbitrary"),
            vmem_limit_bytes=VMEM_LIMIT),
        name="attn",
    )(q, k, v, _tri_matrix())


def _tail_kernel(o_ref, p1_ref, sb_ref, h_ref, mod_ref, wao_ref, wo_ref, g3_ref, wgu_ref, wd_ref,
                 gf_ref, out_ref, acc_ref):
    yb = _dot(o_ref[...], wao_ref[...])
    merged = p1_ref[...].astype(_f32) + sb_ref[...].astype(_f32) * yb
    y = _dot(merged.astype(_bf16), wo_ref[...])
    h2 = h_ref[...] + mod_ref[0, 5:6, :] * y
    u = _norm_mod(h2, g3_ref[...], mod_ref[0, 6:7, :], mod_ref[0, 7:8, :]).astype(_bf16)
    _swiglu_into(acc_ref, u, wgu_ref, wd_ref)
    h3 = h2 + (FFN_RES_WEIGHT * mod_ref[0, 8:9, :]) * acc_ref[...]
    inv = lax.rsqrt(jnp.mean(h3 * h3, axis=-1, keepdims=True) + EPS)
    out_ref[...] = (h3 * inv) * gf_ref[...]


def _tail(o, p1, sb, h, mod, w_attn_out, w_out, norm_g, wgu, wd, final_g, seq):
    t, d = h.shape
    tiles_per_seq = seq // ROW_TILE
    row = lambda width: pl.BlockSpec((ROW_TILE, width), lambda i: (i, 0))
    return pl.pallas_call(
        _tail_kernel,
        grid=(t // ROW_TILE,),
        in_specs=[
            row(ATTN_WIDTH), row(d), row(d), row(d),
            pl.BlockSpec((1, N_SUBLAYERS * N_MOD, d), lambda i: (i // tiles_per_seq, 0, 0)),
            _resident(w_attn_out.shape),
            _resident(w_out.shape),
            _resident((1, d)),
            _resident(wgu.shape),
            _resident(wd.shape),
            _resident((1, d)),
        ],
        out_specs=row(d),
        out_shape=jax.ShapeDtypeStruct((t, d), _f32),
        scratch_shapes=[pltpu.VMEM((ROW_TILE, d), _f32)],
        compiler_params=pltpu.CompilerParams(
            dimension_semantics=("arbitrary",), vmem_limit_bytes=VMEM_LIMIT),
        name="tail",
    )(o, p1, sb, h, mod, w_attn_out, w_out, norm_g, wgu, wd, final_g)


def _pack_gate_up(w_gu):
    d = w_gu.shape[0]
    w = w_gu.astype(_bf16).reshape(d, 2, N_FF_CHUNKS, FF_CHUNK)
    return w.transpose(2, 0, 1, 3).reshape(N_FF_CHUNKS, d, 2 * FF_CHUNK)


def _pack_down(w_down):
    return w_down.astype(_bf16).reshape(N_FF_CHUNKS, FF_CHUNK, w_down.shape[1])


def kernel(x, c, w_ada, b_ada, norm1_g, ffn1_w_gu, ffn1_w_down, norm2_g, w_mix_in, b_merge, conv_w,
           w_conv_out, w_attn_out, w_out, norm3_g, ffn2_w_gu, ffn2_w_down, final_g):
    batch, seq, d = x.shape
    assert w_ada.shape[0] == 1
    assert seq % ROW_TILE == 0 and seq % ATTN_BLOCK == 0 and d == D_MODEL
    h = x.reshape(batch * seq, d)
    mod = _adaln(c, w_ada[0], b_ada[0]).reshape(batch, N_SUBLAYERS * N_MOD, d)
    h1 = _ffn1(h, mod, norm1_g[0].reshape(1, d), _pack_gate_up(ffn1_w_gu[0]),
               _pack_down(ffn1_w_down[0]), seq)
    q, k, v, p1, sb = _mixproj(h1, mod, norm2_g[0].reshape(1, d), w_mix_in[0].astype(_bf16),
                               b_merge[0], conv_w[0], w_conv_out[0].astype(_bf16), seq)
    o = _attention(q.reshape(batch, seq, ATTN_WIDTH), k.reshape(batch, seq, ATTN_WIDTH),
                   v.reshape(batch, seq, ATTN_WIDTH)).reshape(batch * seq, ATTN_WIDTH)
    out = _tail(o, p1, sb, h1, mod, w_attn_out[0].astype(_bf16), w_out[0].astype(_bf16),
                norm3_g[0].reshape(1, d), _pack_gate_up(ffn2_w_gu[0]), _pack_down(ffn2_w_down[0]),
                final_g.reshape(1, d), seq)
    return out.reshape(batch, seq, d)
```

```python
import functools

import jax
import jax.numpy as jnp
from jax import lax
from jax.experimental import pallas as pl
from jax.experimental.pallas import tpu as pltpu

D_MODEL = 1024
N_HEADS = 8
HEAD_DIM = 64
ATTN_WIDTH = N_HEADS * HEAD_DIM
CONV_WIDTH = 512
CONV_KSIZE = 3
D_FF = 2816
N_SUBLAYERS = 3
N_MOD = 3
EPS = 1e-6
FFN_RES_WEIGHT = 0.5
MIX_IN_WIDTH = 3 * CONV_WIDTH + 3 * ATTN_WIDTH + 2 * D_MODEL

LANES = 128
SUBLANES = 8
MXU_DIM = 256
VMEM_LIMIT = 56 * 1024 * 1024

ROW_TILE = 512
FF_CHUNK = MXU_DIM
N_FF_CHUNKS = D_FF // FF_CHUNK
ATTN_BLOCK = 128
SKIP_LOG = 105.0

assert D_FF % FF_CHUNK == 0

_bf16 = jnp.bfloat16
_f32 = jnp.float32


def _dot(a, b):
    return jnp.dot(a, b, preferred_element_type=_f32)


def _split_bf16(x):
    hi = x.astype(_bf16)
    return hi, (x - hi.astype(_f32)).astype(_bf16)


def _resident(shape):
    zeros = (0,) * len(shape)
    return pl.BlockSpec(shape, lambda *_: zeros, pipeline_mode=pl.Buffered(1))


def _norm_mod(x, g, shift, scale):
    inv = lax.rsqrt(jnp.mean(x * x, axis=-1, keepdims=True) + EPS)
    return (x * inv) * g * (1.0 + scale) + shift


def _swiglu_into(acc_ref, u_bf16, wgu_ref, wd_ref):
    for c in range(N_FF_CHUNKS):
        cols = slice(c * FF_CHUNK, (c + 1) * FF_CHUNK)
        up_cols = slice(D_FF + c * FF_CHUNK, D_FF + (c + 1) * FF_CHUNK)
        g = _dot(u_bf16, wgu_ref[:, cols])
        up = _dot(u_bf16, wgu_ref[:, up_cols])
        act = (g * jax.nn.sigmoid(g) * up).astype(_bf16)
        down = _dot(act, wd_ref[cols, :])
        if c == 0:
            acc_ref[...] = down
        else:
            acc_ref[...] += down


def _adaln_kernel(c_ref, w_ref, b_ref, o_ref):
    c = c_ref[...]
    c_act = c * jax.nn.sigmoid(c)
    c_hi, c_lo = _split_bf16(c_act)
    w_hi, w_lo = _split_bf16(w_ref[...])
    o_ref[...] = _dot(c_hi, w_hi) + (_dot(c_lo, w_hi) + _dot(c_hi, w_lo)) + b_ref[...]


def _adaln(c, w_ada, b_ada):
    batch, d = c.shape
    n = w_ada.shape[1]
    bn = 1024
    return pl.pallas_call(
        _adaln_kernel,
        grid=(n // bn,),
        in_specs=[
            pl.BlockSpec((batch, d), lambda j: (0, 0)),
            pl.BlockSpec((d, bn), lambda j: (0, j)),
            pl.BlockSpec((1, bn), lambda j: (0, j)),
        ],
        out_specs=pl.BlockSpec((batch, bn), lambda j: (0, j)),
        out_shape=jax.ShapeDtypeStruct((batch, n), _f32),
        compiler_params=pltpu.CompilerParams(
            dimension_semantics=("arbitrary",), vmem_limit_bytes=VMEM_LIMIT),
        name="adaln",
    )(c, w_ada, b_ada.reshape(1, n))


def _ffn1_kernel(h_ref, mod_ref, g_ref, wgu_ref, wd_ref, o_ref, acc_ref):
    x = h_ref[...]
    shift = mod_ref[0, 0:1, :]
    scale = mod_ref[0, 1:2, :]
    gate = mod_ref[0, 2:3, :]
    u = _norm_mod(x, g_ref[...], shift, scale).astype(_bf16)
    _swiglu_into(acc_ref, u, wgu_ref, wd_ref)
    o_ref[...] = x + (FFN_RES_WEIGHT * gate) * acc_ref[...]


def _ffn1(h, mod, norm_g, wgu, wd, seq):
    t, d = h.shape
    tiles_per_seq = seq // ROW_TILE
    return pl.pallas_call(
        _ffn1_kernel,
        grid=(t // ROW_TILE,),
        in_specs=[
            pl.BlockSpec((ROW_TILE, d), lambda i: (i, 0)),
            pl.BlockSpec((1, N_SUBLAYERS * N_MOD, d), lambda i: (i // tiles_per_seq, 0, 0)),
            _resident((1, d)),
            _resident(wgu.shape),
            _resident(wd.shape),
        ],
        out_specs=pl.BlockSpec((ROW_TILE, d), lambda i: (i, 0)),
        out_shape=jax.ShapeDtypeStruct((t, d), _f32),
        scratch_shapes=[pltpu.VMEM((ROW_TILE, d), _f32)],
        compiler_params=pltpu.CompilerParams(
            dimension_semantics=("arbitrary",), vmem_limit_bytes=VMEM_LIMIT),
        name="ffn1",
    )(h, mod, norm_g, wgu, wd)


def _mixproj_kernel(tiles_per_seq, h_ref, mod_ref, g_ref, w_ref, bm_ref, cw_ref, wco_ref,
                    q_ref, k_ref, v_ref, p1_ref, sb_ref, vbuf_ref):
    i = pl.program_id(0)
    rows = h_ref.shape[0]
    shift = mod_ref[0, 3:4, :]
    scale = mod_ref[0, 4:5, :]
    u = _norm_mod(h_ref[...], g_ref[...], shift, scale).astype(_bf16)

    @pl.when(i % tiles_per_seq == 0)
    def _():
        vbuf_ref[0:SUBLANES, :] = jnp.zeros((SUBLANES, CONV_WIDTH), _f32)

    pc = _dot(u, w_ref[:, 0:3 * CONV_WIDTH])
    cb = pc[:, 0:CONV_WIDTH]
    cv = pc[:, CONV_WIDTH:2 * CONV_WIDTH] * pc[:, 2 * CONV_WIDTH:3 * CONV_WIDTH]
    vbuf_ref[SUBLANES:SUBLANES + rows, :] = cv
    y = (cw_ref[0:1, :] * vbuf_ref[SUBLANES - 2:SUBLANES - 2 + rows, :]
         + cw_ref[1:2, :] * vbuf_ref[SUBLANES - 1:SUBLANES - 1 + rows, :]
         + cw_ref[2:3, :] * cv)
    vbuf_ref[0:SUBLANES, :] = cv[rows - SUBLANES:, :]
    ya = _dot((cb * y).astype(_bf16), wco_ref[...])

    qkv0 = 3 * CONV_WIDTH
    pq = _dot(u, w_ref[:, qkv0:qkv0 + 3 * ATTN_WIDTH])
    q_ref[...] = (pq[:, 0:ATTN_WIDTH] * (HEAD_DIM ** -0.5)).astype(_bf16)
    k_ref[...] = pq[:, ATTN_WIDTH:2 * ATTN_WIDTH].astype(_bf16)
    v_ref[...] = pq[:, 2 * ATTN_WIDTH:3 * ATTN_WIDTH].astype(_bf16)

    g0 = qkv0 + 3 * ATTN_WIDTH
    pg = _dot(u, w_ref[:, g0:g0 + 2 * D_MODEL])
    sa = jax.nn.sigmoid(pg[:, 0:D_MODEL] + bm_ref[0:1, :])
    sb = jax.nn.sigmoid(pg[:, D_MODEL:2 * D_MODEL] + bm_ref[1:2, :])
    p1_ref[...] = (sa * ya).astype(_bf16)
    sb_ref[...] = sb.astype(_bf16)


def _mixproj(h, mod, norm_g, w_mix, b_merge, conv_w, w_conv_out, seq):
    t, d = h.shape
    tiles_per_seq = seq // ROW_TILE
    row = lambda width: pl.BlockSpec((ROW_TILE, width), lambda i: (i, 0))
    return pl.pallas_call(
        functools.partial(_mixproj_kernel, tiles_per_seq),
        grid=(t // ROW_TILE,),
        in_specs=[
            row(d),
            pl.BlockSpec((1, N_SUBLAYERS * N_MOD, d), lambda i: (i // tiles_per_seq, 0, 0)),
            _resident((1, d)),
            _resident(w_mix.shape),
            _resident(b_merge.shape),
            _resident(conv_w.shape),
            _resident(w_conv_out.shape),
        ],
        out_specs=[row(ATTN_WIDTH), row(ATTN_WIDTH), row(ATTN_WIDTH), row(d), row(d)],
        out_shape=[jax.ShapeDtypeStruct((t, ATTN_WIDTH), _bf16)] * 3
        + [jax.ShapeDtypeStruct((t, d), _bf16)] * 2,
        scratch_shapes=[pltpu.VMEM((SUBLANES + ROW_TILE, CONV_WIDTH), _f32)],
        compiler_params=pltpu.CompilerParams(
            dimension_semantics=("arbitrary",), vmem_limit_bytes=VMEM_LIMIT),
        name="mixproj",
    )(h, mod, norm_g, w_mix, b_merge, conv_w, w_conv_out)


def _attn_kernel(q_ref, k_ref, v_ref, tri_ref, o_ref, acc_ref, rem_ref):
    iq = pl.program_id(1)
    bq = ATTN_BLOCK
    n_pairs = q_ref.shape[2] // LANES
    lane = lax.broadcasted_iota(jnp.int32, (bq, LANES), 1).astype(_f32).astype(_bf16)
    first_head = lane < HEAD_DIM

    def split_heads(x):
        zero = jnp.zeros_like(x)
        return jnp.concatenate([jnp.where(first_head, x, zero), jnp.where(first_head, zero, x)], axis=0)

    row = lax.broadcasted_iota(jnp.int32, (2 * bq, bq), 0)
    col = lax.broadcasted_iota(jnp.int32, (2 * bq, bq), 1)
    strictly_causal = col < jnp.where(row >= bq, row - bq, row)

    acc_ref[...] = jnp.zeros_like(acc_ref)
    rem_ref[...] = jnp.zeros_like(rem_ref)

    pairs = range(n_pairs)
    lanes = [slice(p * LANES, (p + 1) * LANES) for p in pairs]
    q2 = [split_heads(q_ref[0, :, lanes[p]]) for p in pairs]

    def visit(blocks):
        starts = [pl.multiple_of(j * bq, bq) for j, _ in blocks]
        zs = [[lax.dot_general(q2[p], k_ref[0, pl.ds(start, bq), lanes[p]],
                               (((1,), (1,)), ((), ())), preferred_element_type=_f32) for p in pairs]
              for start in starts]
        hilo = []
        for (_, diagonal), z_block in zip(blocks, zs):
            row_hilo = []
            for z in z_block:
                sp = jnp.maximum(z, 0.0) + jnp.log(1.0 + jnp.exp(-jnp.abs(z)))
                if diagonal:
                    sp = jnp.where(strictly_causal, sp, 0.0)
                row_hilo.append(jnp.concatenate(_split_bf16(sp), axis=1))
            hilo.append(row_hilo)
        css = [[_dot(x, tri_ref[...]) for x in row_hilo] for row_hilo in hilo]
        seen = [rem_ref[p] for p in pairs]
        weights = []
        for n, (_, diagonal) in enumerate(blocks):
            row_w = []
            for p in pairs:
                a = jnp.exp(zs[n][p] - css[n][p][:, :bq] - seen[p])
                if diagonal:
                    a = jnp.where(strictly_causal, a, 0.0)
                a = a.astype(_bf16)
                row_w.append(jnp.concatenate([a[:bq], a[bq:]], axis=1))
                seen[p] = seen[p] + css[n][p][:, bq:]
            weights.append(row_w)
        for p in pairs:
            out = None
            for n, start in enumerate(starts):
                term = _dot(weights[n][p], split_heads(v_ref[0, pl.ds(start, bq), lanes[p]]))
                out = term if out is None else out + term
            acc_ref[p] += out
            rem_ref[p] = seen[p]

    fused = iq >= 2

    @pl.when(fused)
    def _():
        visit([(iq, True), (iq - 1, False), (iq - 2, False)])

    @pl.when(jnp.logical_not(fused))
    def _():
        visit([(iq, True)])

    def unfinished():
        return jnp.min(rem_ref[...]) < SKIP_LOG

    def cond(state):
        j, live = state
        return jnp.logical_and(j >= 0, live)

    def body(state):
        j, _ = state
        visit([(j, False)])
        return j - 1, unfinished()

    lax.while_loop(cond, body, (jnp.where(fused, iq - 3, iq - 1), unfinished()))
    for p in range(n_pairs):
        o_ref[0, :, p * LANES:(p + 1) * LANES] = acc_ref[p].astype(o_ref.dtype)


def _tri_matrix():
    bq = ATTN_BLOCK
    j = lax.broadcasted_iota(jnp.int32, (bq, bq), 0)
    s = lax.broadcasted_iota(jnp.int32, (bq, bq), 1)
    tri = (j >= s).astype(_bf16)
    half = jnp.concatenate([tri, jnp.ones((bq, bq), _bf16)], axis=1)
    return jnp.concatenate([half, half], axis=0)


def _attention(q, k, v):
    batch, seq, width = q.shape
    n_pairs = width // LANES
    bq = ATTN_BLOCK
    blk = pl.BlockSpec((1, bq, width), lambda b, i: (b, i, 0))
    whole = pl.BlockSpec((1, seq, width), lambda b, i: (b, 0, 0))
    return pl.pallas_call(
        _attn_kernel,
        grid=(batch, seq // bq),
        in_specs=[blk, whole, whole, _resident((2 * bq, 2 * bq))],
        out_specs=blk,
        out_shape=jax.ShapeDtypeStruct((batch, seq, width), _bf16),
        scratch_shapes=[pltpu.VMEM((n_pairs, bq, LANES), _f32),
                        pltpu.VMEM((n_pairs, 2 * bq, bq), _f32)],
        compiler_params=pltpu.CompilerParams(
            dimension_semantics=("arbitrary", "arbitrary"), vmem_limit_bytes=VMEM_LIMIT),
        name="attn",
    )(q, k, v, _tri_matrix())


def _tail_kernel(o_ref, p1_ref, sb_ref, h_ref, mod_ref, wao_ref, wo_ref, g3_ref, wgu_ref, wd_ref,
                 gf_ref, out_ref, acc_ref):
    yb = _dot(o_ref[...], wao_ref[...])
    merged = p1_ref[...].astype(_f32) + sb_ref[...].astype(_f32) * yb
    y = _dot(merged.astype(_bf16), wo_ref[...])
    h2 = h_ref[...] + mod_ref[0, 5:6, :] * y
    u = _norm_mod(h2, g3_ref[...], mod_ref[0, 6:7, :], mod_ref[0, 7:8, :]).astype(_bf16)
    _swiglu_into(acc_ref, u, wgu_ref, wd_ref)
    h3 = h2 + (FFN_RES_WEIGHT * mod_ref[0, 8:9, :]) * acc_ref[...]
    inv = lax.rsqrt(jnp.mean(h3 * h3, axis=-1, keepdims=True) + EPS)
    out_ref[...] = (h3 * inv) * gf_ref[...]


def _tail(o, p1, sb, h, mod, w_attn_out, w_out, norm_g, wgu, wd, final_g, seq):
    t, d = h.shape
    tiles_per_seq = seq // ROW_TILE
    row = lambda width: pl.BlockSpec((ROW_TILE, width), lambda i: (i, 0))
    return pl.pallas_call(
        _tail_kernel,
        grid=(t // ROW_TILE,),
        in_specs=[
            row(ATTN_WIDTH), row(d), row(d), row(d),
            pl.BlockSpec((1, N_SUBLAYERS * N_MOD, d), lambda i: (i // tiles_per_seq, 0, 0)),
            _resident(w_attn_out.shape),
            _resident(w_out.shape),
            _resident((1, d)),
            _resident(wgu.shape),
            _resident(wd.shape),
            _resident((1, d)),
        ],
        out_specs=row(d),
        out_shape=jax.ShapeDtypeStruct((t, d), _f32),
        scratch_shapes=[pltpu.VMEM((ROW_TILE, d), _f32)],
        compiler_params=pltpu.CompilerParams(
            dimension_semantics=("arbitrary",), vmem_limit_bytes=VMEM_LIMIT),
        name="tail",
    )(o, p1, sb, h, mod, w_attn_out, w_out, norm_g, wgu, wd, final_g)


def kernel(x, c, w_ada, b_ada, norm1_g, ffn1_w_gu, ffn1_w_down, norm2_g, w_mix_in, b_merge, conv_w,
           w_conv_out, w_attn_out, w_out, norm3_g, ffn2_w_gu, ffn2_w_down, final_g):
    batch, seq, d = x.shape
    assert w_ada.shape[0] == 1
    assert seq % ROW_TILE == 0 and seq % ATTN_BLOCK == 0 and d == D_MODEL
    h = x.reshape(batch * seq, d)
    mod = _adaln(c, w_ada[0], b_ada[0]).reshape(batch, N_SUBLAYERS * N_MOD, d)
    h1 = _ffn1(h, mod, norm1_g[0].reshape(1, d), ffn1_w_gu[0].astype(_bf16),
               ffn1_w_down[0].astype(_bf16), seq)
    q, k, v, p1, sb = _mixproj(h1, mod, norm2_g[0].reshape(1, d), w_mix_in[0].astype(_bf16),
                               b_merge[0], conv_w[0], w_conv_out[0].astype(_bf16), seq)
    o = _attention(q.reshape(batch, seq, ATTN_WIDTH), k.reshape(batch, seq, ATTN_WIDTH),
                   v.reshape(batch, seq, ATTN_WIDTH)).reshape(batch * seq, ATTN_WIDTH)
    out = _tail(o, p1, sb, h1, mod, w_attn_out[0].astype(_bf16), w_out[0].astype(_bf16),
                norm3_g[0].reshape(1, d), ffn2_w_gu[0].astype(_bf16), ffn2_w_down[0].astype(_bf16),
                final_g.reshape(1, d), seq)
    return out.reshape(batch, seq, d)
```

```python
import functools

import jax
import jax.numpy as jnp
from jax import lax
from jax.experimental import pallas as pl
from jax.experimental.pallas import tpu as pltpu

D_MODEL = 1024
N_HEADS = 8
HEAD_DIM = 64
ATTN_WIDTH = N_HEADS * HEAD_DIM
CONV_WIDTH = 512
CONV_KSIZE = 3
D_FF = 2816
N_SUBLAYERS = 3
N_MOD = 3
EPS = 1e-6
FFN_RES_WEIGHT = 0.5
MIX_IN_WIDTH = 3 * CONV_WIDTH + 3 * ATTN_WIDTH + 2 * D_MODEL

LANES = 128
SUBLANES = 8
BF16_SUBLANES = 16
MXU_DIM = 256
VMEM_LIMIT = 56 * 1024 * 1024

ROW_TILE = 512
FF_CHUNK = MXU_DIM
N_FF_CHUNKS = D_FF // FF_CHUNK
ATTN_BLOCK = 128
SKIP_LOG = 105.0

assert D_FF % FF_CHUNK == 0

_bf16 = jnp.bfloat16
_f32 = jnp.float32


def _dot(a, b):
    return jnp.dot(a, b, preferred_element_type=_f32)


def _split_bf16(x):
    hi = x.astype(_bf16)
    return hi, (x - hi.astype(_f32)).astype(_bf16)


def _resident(shape):
    zeros = (0,) * len(shape)
    return pl.BlockSpec(shape, lambda *_: zeros, pipeline_mode=pl.Buffered(1))


def _norm_mod(x, g, shift, scale):
    inv = lax.rsqrt(jnp.mean(x * x, axis=-1, keepdims=True) + EPS)
    return (x * inv) * g * (1.0 + scale) + shift


def _swiglu_chunk(acc_ref, u_bf16, wgu_ref, wd_ref, c):
    cols = slice(c * FF_CHUNK, (c + 1) * FF_CHUNK)
    up_cols = slice(D_FF + c * FF_CHUNK, D_FF + (c + 1) * FF_CHUNK)
    g = _dot(u_bf16, wgu_ref[:, cols])
    up = _dot(u_bf16, wgu_ref[:, up_cols])
    act = (g * jax.nn.sigmoid(g) * up).astype(_bf16)
    down = _dot(act, wd_ref[cols, :])
    if c == 0:
        acc_ref[...] = down
    else:
        acc_ref[...] += down


def _swiglu_into(acc_ref, u_bf16, wgu_ref, wd_ref):
    for c in range(N_FF_CHUNKS):
        _swiglu_chunk(acc_ref, u_bf16, wgu_ref, wd_ref, c)


def _adaln_kernel(c_ref, w_ref, b_ref, o_ref):
    c = c_ref[...]
    c_act = c * jax.nn.sigmoid(c)
    c_hi, c_lo = _split_bf16(c_act)
    w_hi, w_lo = _split_bf16(w_ref[...])
    o_ref[...] = _dot(c_hi, w_hi) + (_dot(c_lo, w_hi) + _dot(c_hi, w_lo)) + b_ref[...]


def _adaln(c, w_ada, b_ada):
    batch, d = c.shape
    n = w_ada.shape[1]
    bn = 1024
    return pl.pallas_call(
        _adaln_kernel,
        grid=(n // bn,),
        in_specs=[
            pl.BlockSpec((batch, d), lambda j: (0, 0)),
            pl.BlockSpec((d, bn), lambda j: (0, j)),
            pl.BlockSpec((1, bn), lambda j: (0, j)),
        ],
        out_specs=pl.BlockSpec((batch, bn), lambda j: (0, j)),
        out_shape=jax.ShapeDtypeStruct((batch, n), _f32),
        compiler_params=pltpu.CompilerParams(
            dimension_semantics=("arbitrary",), vmem_limit_bytes=VMEM_LIMIT),
        name="adaln",
    )(c, w_ada, b_ada.reshape(1, n))


def _ffn1_kernel(n_cast, h_ref, mod_ref, g_ref, wgu_ref, wd_ref, *rest):
    cast_in = rest[:n_cast]
    o_ref = rest[n_cast]
    cast_out = rest[n_cast + 1:2 * n_cast + 1]
    acc_ref = rest[2 * n_cast + 1]
    for src, dst in zip(cast_in, cast_out):
        dst[...] = src[...].astype(_bf16)
    x = h_ref[...]
    shift = mod_ref[0, 0:1, :]
    scale = mod_ref[0, 1:2, :]
    gate = mod_ref[0, 2:3, :]
    u = _norm_mod(x, g_ref[...], shift, scale).astype(_bf16)
    _swiglu_into(acc_ref, u, wgu_ref, wd_ref)
    o_ref[...] = x + (FFN_RES_WEIGHT * gate) * acc_ref[...]


def _cast_slab_spec(shape, n_steps):
    rows, cols = shape
    span = 1
    while (rows * span) % (n_steps * BF16_SUBLANES) != 0:
        span *= 2
    return pl.BlockSpec((rows * span // n_steps, cols), lambda i: (i // span, 0))


def _ffn1(h, mod, norm_g, wgu, wd, later_weights, seq):
    t, d = h.shape
    step = ROW_TILE
    tiles_per_seq = seq // step
    n_steps = t // step
    cast_specs = [_cast_slab_spec(w.shape, n_steps) for w in later_weights]
    row_spec = pl.BlockSpec((step, d), lambda i: (i, 0))
    outs = pl.pallas_call(
        functools.partial(_ffn1_kernel, len(later_weights)),
        grid=(n_steps,),
        in_specs=[
            row_spec,
            pl.BlockSpec((1, N_SUBLAYERS * N_MOD, d), lambda i: (i // tiles_per_seq, 0, 0)),
            _resident((1, d)),
            _resident(wgu.shape),
            _resident(wd.shape),
        ] + cast_specs,
        out_specs=[row_spec] + cast_specs,
        out_shape=[jax.ShapeDtypeStruct((t, d), _f32)]
        + [jax.ShapeDtypeStruct(w.shape, _bf16) for w in later_weights],
        scratch_shapes=[pltpu.VMEM((step, d), _f32)],
        compiler_params=pltpu.CompilerParams(
            dimension_semantics=("arbitrary",), vmem_limit_bytes=VMEM_LIMIT),
        name="ffn1",
    )(h, mod, norm_g, wgu, wd, *later_weights)
    return outs[0], outs[1:]


def _mixproj_kernel(tiles_per_seq, h_ref, mod_ref, g_ref, w_ref, bm_ref, cw_ref, wco_ref,
                    q_ref, k_ref, v_ref, p1_ref, sb_ref, vbuf_ref):
    i = pl.program_id(0)
    rows = h_ref.shape[0]
    shift = mod_ref[0, 3:4, :]
    scale = mod_ref[0, 4:5, :]
    u = _norm_mod(h_ref[...], g_ref[...], shift, scale).astype(_bf16)

    @pl.when(i % tiles_per_seq == 0)
    def _():
        vbuf_ref[0:SUBLANES, :] = jnp.zeros((SUBLANES, CONV_WIDTH), _f32)

    qkv0 = 3 * CONV_WIDTH
    g0 = qkv0 + 3 * ATTN_WIDTH
    pc = _dot(u, w_ref[:, 0:qkv0])
    pq = _dot(u, w_ref[:, qkv0:g0])

    cb = pc[:, 0:CONV_WIDTH]
    cv = pc[:, CONV_WIDTH:2 * CONV_WIDTH] * pc[:, 2 * CONV_WIDTH:3 * CONV_WIDTH]
    vbuf_ref[SUBLANES:SUBLANES + rows, :] = cv
    y = (cw_ref[0:1, :] * vbuf_ref[SUBLANES - 2:SUBLANES - 2 + rows, :]
         + cw_ref[1:2, :] * vbuf_ref[SUBLANES - 1:SUBLANES - 1 + rows, :]
         + cw_ref[2:3, :] * cv)
    vbuf_ref[0:SUBLANES, :] = cv[rows - SUBLANES:, :]
    conv = (cb * y).astype(_bf16)

    pga = _dot(u, w_ref[:, g0:g0 + D_MODEL])
    q_ref[...] = (pq[:, 0:ATTN_WIDTH] * (HEAD_DIM ** -0.5)).astype(_bf16)
    k_ref[...] = pq[:, ATTN_WIDTH:2 * ATTN_WIDTH].astype(_bf16)
    v_ref[...] = pq[:, 2 * ATTN_WIDTH:3 * ATTN_WIDTH].astype(_bf16)

    pgb = _dot(u, w_ref[:, g0 + D_MODEL:g0 + 2 * D_MODEL])
    sa = jax.nn.sigmoid(pga + bm_ref[0:1, :])
    ya = _dot(conv, wco_ref[...])
    sb_ref[...] = jax.nn.sigmoid(pgb + bm_ref[1:2, :]).astype(_bf16)
    p1_ref[...] = (sa * ya).astype(_bf16)


def _mixproj(h, mod, norm_g, w_mix, b_merge, conv_w, w_conv_out, seq):
    t, d = h.shape
    tiles_per_seq = seq // ROW_TILE
    row = lambda width: pl.BlockSpec((ROW_TILE, width), lambda i: (i, 0))
    return pl.pallas_call(
        functools.partial(_mixproj_kernel, tiles_per_seq),
        grid=(t // ROW_TILE,),
        in_specs=[
            row(d),
            pl.BlockSpec((1, N_SUBLAYERS * N_MOD, d), lambda i: (i // tiles_per_seq, 0, 0)),
            _resident((1, d)),
            _resident(w_mix.shape),
            _resident(b_merge.shape),
            _resident(conv_w.shape),
            _resident(w_conv_out.shape),
        ],
        out_specs=[row(ATTN_WIDTH), row(ATTN_WIDTH), row(ATTN_WIDTH), row(d), row(d)],
        out_shape=[jax.ShapeDtypeStruct((t, ATTN_WIDTH), _bf16)] * 3
        + [jax.ShapeDtypeStruct((t, d), _bf16)] * 2,
        scratch_shapes=[pltpu.VMEM((SUBLANES + ROW_TILE, CONV_WIDTH), _f32)],
        compiler_params=pltpu.CompilerParams(
            dimension_semantics=("arbitrary",), vmem_limit_bytes=VMEM_LIMIT),
        name="mixproj",
    )(h, mod, norm_g, w_mix, b_merge, conv_w, w_conv_out)


def _attn_kernel(q_ref, k_ref, v_ref, tri_ref, o_ref, acc_ref, rem_ref):
    iq = pl.program_id(1)
    bq = ATTN_BLOCK
    n_pairs = q_ref.shape[2] // LANES
    lane = lax.broadcasted_iota(jnp.int32, (bq, LANES), 1).astype(_f32).astype(_bf16)
    first_head = lane < HEAD_DIM

    def split_heads(x):
        zero = jnp.zeros_like(x)
        return jnp.concatenate([jnp.where(first_head, x, zero), jnp.where(first_head, zero, x)], axis=0)

    row = lax.broadcasted_iota(jnp.int32, (2 * bq, bq), 0)
    col = lax.broadcasted_iota(jnp.int32, (2 * bq, bq), 1)
    strictly_causal = col < jnp.where(row >= bq, row - bq, row)

    pairs = range(n_pairs)
    lanes = [slice(p * LANES, (p + 1) * LANES) for p in pairs]
    q2 = [split_heads(q_ref[0, :, lanes[p]]) for p in pairs]

    def scores(p, start, n_blocks):
        return lax.dot_general(q2[p], k_ref[0, pl.ds(start, n_blocks * bq), lanes[p]],
                               (((1,), (1,)), ((), ())), preferred_element_type=_f32)

    def visit(blocks, first):
        starts = [pl.multiple_of(j * bq, bq) for j, _ in blocks]
        if len(blocks) == 3:
            both = [scores(p, starts[2], 2) for p in pairs]
            zs = [[scores(p, starts[0], 1) for p in pairs],
                  [z[:, bq:] for z in both], [z[:, :bq] for z in both]]
        else:
            zs = [[scores(p, start, 1) for p in pairs] for start in starts]
        hilo = []
        for (_, diagonal), z_block in zip(blocks, zs):
            row_hilo = []
            for z in z_block:
                sp = jnp.maximum(z, 0.0) + jnp.log(1.0 + jnp.exp(-jnp.abs(z)))
                if diagonal:
                    sp = jnp.where(strictly_causal, sp, 0.0)
                row_hilo.append(jnp.concatenate(_split_bf16(sp), axis=1))
            hilo.append(row_hilo)
        css = [[_dot(x, tri_ref[...]) for x in row_hilo] for row_hilo in hilo]
        seen = [None if first else rem_ref[p] for p in pairs]
        weights = []
        for n, (_, diagonal) in enumerate(blocks):
            row_w = []
            for p in pairs:
                log_a = zs[n][p] - css[n][p][:, :bq]
                a = jnp.exp(log_a if seen[p] is None else log_a - seen[p])
                if diagonal:
                    a = jnp.where(strictly_causal, a, 0.0)
                a = a.astype(_bf16)
                row_w.append(jnp.concatenate([a[:bq], a[bq:]], axis=1))
                total = css[n][p][:, bq:]
                seen[p] = total if seen[p] is None else seen[p] + total
            weights.append(row_w)
        for p in pairs:
            out = None
            for n, start in enumerate(starts):
                term = _dot(weights[n][p], split_heads(v_ref[0, pl.ds(start, bq), lanes[p]]))
                out = term if out is None else out + term
            if first:
                acc_ref[p] = out
            else:
                acc_ref[p] += out
            rem_ref[p] = seen[p]

    fused = iq >= 2

    @pl.when(fused)
    def _():
        visit([(iq, True), (iq - 1, False), (iq - 2, False)], first=True)

    @pl.when(jnp.logical_not(fused))
    def _():
        visit([(iq, True)], first=True)

    def unfinished():
        return jnp.min(rem_ref[...]) < SKIP_LOG

    def cond(state):
        j, live = state
        return jnp.logical_and(j >= 0, live)

    def body(state):
        j, _ = state
        visit([(j, False)], first=False)
        return j - 1, unfinished()

    lax.while_loop(cond, body, (jnp.where(fused, iq - 3, iq - 1), unfinished()))
    for p in range(n_pairs):
        o_ref[0, :, p * LANES:(p + 1) * LANES] = acc_ref[p].astype(o_ref.dtype)


def _tri_matrix():
    bq = ATTN_BLOCK
    j = lax.broadcasted_iota(jnp.int32, (bq, bq), 0)
    s = lax.broadcasted_iota(jnp.int32, (bq, bq), 1)
    tri = (j >= s).astype(_bf16)
    half = jnp.concatenate([tri, jnp.ones((bq, bq), _bf16)], axis=1)
    return jnp.concatenate([half, half], axis=0)


def _attention(q, k, v):
    batch, seq, width = q.shape
    n_pairs = width // LANES
    bq = ATTN_BLOCK
    blk = pl.BlockSpec((1, bq, width), lambda b, i: (b, i, 0))
    whole = pl.BlockSpec((1, seq, width), lambda b, i: (b, 0, 0))
    return pl.pallas_call(
        _attn_kernel,
        grid=(batch, seq // bq),
        in_specs=[blk, whole, whole, _resident((2 * bq, 2 * bq))],
        out_specs=blk,
        out_shape=jax.ShapeDtypeStruct((batch, seq, width), _bf16),
        scratch_shapes=[pltpu.VMEM((n_pairs, bq, LANES), _f32),
                        pltpu.VMEM((n_pairs, 2 * bq, bq), _f32)],
        compiler_params=pltpu.CompilerParams(
            dimension_semantics=("arbitrary", "arbitrary"), vmem_limit_bytes=VMEM_LIMIT),
        name="attn",
    )(q, k, v, _tri_matrix())


def _tail_kernel(o_ref, p1_ref, sb_ref, h_ref, mod_ref, wao_ref, wo_ref, g3_ref, wgu_ref, wd_ref,
                 gf_ref, out_ref, acc_ref):
    yb = _dot(o_ref[...], wao_ref[...])
    merged = p1_ref[...].astype(_f32) + sb_ref[...].astype(_f32) * yb
    y = _dot(merged.astype(_bf16), wo_ref[...])
    h2 = h_ref[...] + mod_ref[0, 5:6, :] * y
    u = _norm_mod(h2, g3_ref[...], mod_ref[0, 6:7, :], mod_ref[0, 7:8, :]).astype(_bf16)
    _swiglu_into(acc_ref, u, wgu_ref, wd_ref)
    h3 = h2 + (FFN_RES_WEIGHT * mod_ref[0, 8:9, :]) * acc_ref[...]
    inv = lax.rsqrt(jnp.mean(h3 * h3, axis=-1, keepdims=True) + EPS)
    out_ref[...] = (h3 * inv) * gf_ref[...]


def _tail(o, p1, sb, h, mod, w_attn_out, w_out, norm_g, wgu, wd, final_g, seq):
    t, d = h.shape
    tiles_per_seq = seq // ROW_TILE
    row = lambda width: pl.BlockSpec((ROW_TILE, width), lambda i: (i, 0))
    return pl.pallas_call(
        _tail_kernel,
        grid=(t // ROW_TILE,),
        in_specs=[
            row(ATTN_WIDTH), row(d), row(d), row(d),
            pl.BlockSpec((1, N_SUBLAYERS * N_MOD, d), lambda i: (i // tiles_per_seq, 0, 0)),
            _resident(w_attn_out.shape),
            _resident(w_out.shape),
            _resident((1, d)),
            _resident(wgu.shape),
            _resident(wd.shape),
            _resident((1, d)),
        ],
        out_specs=row(d),
        out_shape=jax.ShapeDtypeStruct((t, d), _f32),
        scratch_shapes=[pltpu.VMEM((ROW_TILE, d), _f32)],
        compiler_params=pltpu.CompilerParams(
            dimension_semantics=("arbitrary",), vmem_limit_bytes=VMEM_LIMIT),
        name="tail",
    )(o, p1, sb, h, mod, w_attn_out, w_out, norm_g, wgu, wd, final_g)


def kernel(x, c, w_ada, b_ada, norm1_g, ffn1_w_gu, ffn1_w_down, norm2_g, w_mix_in, b_merge, conv_w,
           w_conv_out, w_attn_out, w_out, norm3_g, ffn2_w_gu, ffn2_w_down, final_g):
    batch, seq, d = x.shape
    assert w_ada.shape[0] == 1
    assert seq % ROW_TILE == 0 and seq % ATTN_BLOCK == 0 and d == D_MODEL
    h = x.reshape(batch * seq, d)
    mod = _adaln(c, w_ada[0], b_ada[0]).reshape(batch, N_SUBLAYERS * N_MOD, d)
    h1, (w_mix, w_co, w_ao, w_o, w_gu2, w_d2) = _ffn1(
        h, mod, norm1_g[0].reshape(1, d), ffn1_w_gu[0].astype(_bf16), ffn1_w_down[0].astype(_bf16),
        [w_mix_in[0], w_conv_out[0], w_attn_out[0], w_out[0], ffn2_w_gu[0], ffn2_w_down[0]], seq)
    q, k, v, p1, sb = _mixproj(h1, mod, norm2_g[0].reshape(1, d), w_mix, b_merge[0], conv_w[0],
                               w_co, seq)
    o = _attention(q.reshape(batch, seq, ATTN_WIDTH), k.reshape(batch, seq, ATTN_WIDTH),
                   v.reshape(batch, seq, ATTN_WIDTH)).reshape(batch * seq, ATTN_WIDTH)
    out = _tail(o, p1, sb, h1, mod, w_ao, w_o, norm3_g[0].reshape(1, d), w_gu2, w_d2,
                final_g.reshape(1, d), seq)
    return out.reshape(batch, seq, d)
```

```python
import functools

import jax
import jax.numpy as jnp
from jax import lax
from jax.experimental import pallas as pl
from jax.experimental.pallas import tpu as pltpu

D_MODEL = 1024
N_HEADS = 8
HEAD_DIM = 64
ATTN_WIDTH = N_HEADS * HEAD_DIM
CONV_WIDTH = 512
CONV_KSIZE = 3
D_FF = 2816
N_SUBLAYERS = 3
N_MOD = 3
EPS = 1e-6
FFN_RES_WEIGHT = 0.5
MIX_IN_WIDTH = 3 * CONV_WIDTH + 3 * ATTN_WIDTH + 2 * D_MODEL

LANES = 128
SUBLANES = 8
BF16_SUBLANES = 16
MXU_DIM = 256
VMEM_LIMIT = 56 * 1024 * 1024

ROW_TILE = 512
PIECE_ROWS = 64
FF_CHUNK = MXU_DIM
N_FF_CHUNKS = D_FF // FF_CHUNK
ATTN_BLOCK = 128
SKIP_LOG = 105.0

assert D_FF % FF_CHUNK == 0

_bf16 = jnp.bfloat16
_f32 = jnp.float32


def _runtime_zero():
    return jnp.zeros((1,), jnp.int32)


def _dot(a, b):
    return jnp.dot(a, b, preferred_element_type=_f32)


def _split_bf16(x):
    hi = x.astype(_bf16)
    return hi, (x - hi.astype(_f32)).astype(_bf16)


def _resident(shape):
    zeros = (0,) * len(shape)
    return pl.BlockSpec(shape, lambda *_: zeros, pipeline_mode=pl.Buffered(1))


def _norm_mod(x, gain, shift):
    inv = lax.rsqrt(jnp.mean(x * x, axis=-1, keepdims=True) + EPS)
    return (x * inv) * gain + shift


def _gain(g_ref, mod_ref, sublayer):
    return g_ref[...] * (1.0 + mod_ref[0, N_MOD * sublayer + 1:N_MOD * sublayer + 2, :])


def _shift(mod_ref, sublayer):
    return mod_ref[0, N_MOD * sublayer:N_MOD * sublayer + 1, :]


def _gate(mod_ref, sublayer):
    return mod_ref[0, N_MOD * sublayer + 2:N_MOD * sublayer + 3, :]


def _token(value):
    return value[0:1, 0:LANES].astype(_f32)


def _tie(never, token, value):
    head = jnp.where(never, token, value[:, :LANES])
    if value.shape[1] == LANES:
        return head
    return jnp.concatenate([head, value[:, LANES:]], axis=1)


def _run_jobs(token, jobs):
    for job in jobs:
        token = job(token)
    return token


def _next_norm_jobs(never, hn_ref, un_ref, gain, shift):
    def job(k):
        rows = slice(k * PIECE_ROWS, (k + 1) * PIECE_ROWS)

        def run(token):
            piece = _norm_mod(hn_ref[rows, :], _tie(never, token, gain), shift).astype(_bf16)
            un_ref[rows, :] = piece
            return _token(piece)
        return run
    return [job(k) for k in range(ROW_TILE // PIECE_ROWS)]


def _swiglu_into(acc_ref, u_bf16, wgu_ref, wd_ref, never=None, side_jobs=()):
    pending = {}
    for c in range(N_FF_CHUNKS):
        cols = slice(c * FF_CHUNK, (c + 1) * FF_CHUNK)
        up_cols = slice(D_FF + c * FF_CHUNK, D_FF + (c + 1) * FF_CHUNK)
        g = _dot(u_bf16, wgu_ref[:, cols])
        if c - 2 in pending:
            g = _tie(never, pending.pop(c - 2), g)
        if c < len(side_jobs) and side_jobs[c]:
            pending[c] = _run_jobs(_token(g), side_jobs[c])
        up = _dot(u_bf16, wgu_ref[:, up_cols])
        act = (g * jax.nn.sigmoid(g) * up).astype(_bf16)
        down = _dot(act, wd_ref[cols, :])
        if c == 0:
            acc_ref[...] = down
        else:
            acc_ref[...] += down
    return list(pending.values())


def _adaln_kernel(c_ref, w_ref, b_ref, o_ref):
    c = c_ref[...]
    c_act = c * jax.nn.sigmoid(c)
    c_hi, c_lo = _split_bf16(c_act)
    w_hi, w_lo = _split_bf16(w_ref[...])
    o_ref[...] = _dot(c_hi, w_hi) + (_dot(c_lo, w_hi) + _dot(c_hi, w_lo)) + b_ref[...]


def _adaln(c, w_ada, b_ada):
    batch, d = c.shape
    n = w_ada.shape[1]
    bn = 1024
    return pl.pallas_call(
        _adaln_kernel,
        grid=(n // bn,),
        in_specs=[
            pl.BlockSpec((batch, d), lambda j: (0, 0)),
            pl.BlockSpec((d, bn), lambda j: (0, j)),
            pl.BlockSpec((1, bn), lambda j: (0, j)),
        ],
        out_specs=pl.BlockSpec((batch, bn), lambda j: (0, j)),
        out_shape=jax.ShapeDtypeStruct((batch, n), _f32),
        compiler_params=pltpu.CompilerParams(
            dimension_semantics=("arbitrary",), vmem_limit_bytes=VMEM_LIMIT),
        name="adaln",
    )(c, w_ada, b_ada.reshape(1, n))


def _ffn1_kernel(n_cast, zero_ref, h_ref, hn_ref, mod_ref, modn_ref, g_ref, wgu_ref, wd_ref, *rest):
    cast_in = rest[:n_cast]
    o_ref = rest[n_cast]
    cast_out = rest[n_cast + 1:2 * n_cast + 1]
    acc_ref, u_even_ref, u_odd_ref = rest[2 * n_cast + 1:]
    i = pl.program_id(0)
    never = zero_ref[0] != 0

    @pl.when(i == 0)
    def _():
        u_even_ref[...] = _norm_mod(h_ref[...], _gain(g_ref, mod_ref, 0), _shift(mod_ref, 0)).astype(_bf16)

    def cast_job(src, dst):
        def run(token):
            w = _tie(never, token, src[...]).astype(_bf16)
            dst[...] = w
            return _token(w)
        return run

    def step(u_ref, un_ref):
        norm_jobs = _next_norm_jobs(never, hn_ref, un_ref, _gain(g_ref, modn_ref, 0), _shift(modn_ref, 0))
        cast_jobs = [cast_job(src, dst) for src, dst in zip(cast_in, cast_out)]
        n_tail = N_FF_CHUNKS - len(norm_jobs)
        side_jobs = [[job] for job in norm_jobs] + [cast_jobs[r::n_tail] for r in range(n_tail)]
        tokens = _swiglu_into(acc_ref, u_ref[...], wgu_ref, wd_ref, never, side_jobs)
        acc = acc_ref[...]
        for token in tokens:
            acc = _tie(never, token, acc)
        o_ref[...] = h_ref[...] + (FFN_RES_WEIGHT * _gate(mod_ref, 0)) * acc

    @pl.when(i % 2 == 0)
    def _():
        step(u_even_ref, u_odd_ref)

    @pl.when(i % 2 == 1)
    def _():
        step(u_odd_ref, u_even_ref)


def _cast_slab_spec(shape, n_steps):
    rows, cols = shape
    span = 1
    while (rows * span) % (n_steps * BF16_SUBLANES) != 0:
        span *= 2
    return pl.BlockSpec((rows * span // n_steps, cols), lambda i: (i // span, 0))


def _ffn1(h, mod, norm_g, wgu, wd, later_weights, seq):
    t, d = h.shape
    step = ROW_TILE
    tiles_per_seq = seq // step
    n_steps = t // step
    cast_specs = [_cast_slab_spec(w.shape, n_steps) for w in later_weights]
    row_spec = pl.BlockSpec((step, d), lambda i: (i, 0))
    nxt = lambda i: jnp.minimum(i + 1, n_steps - 1)
    mod_block = (1, N_SUBLAYERS * N_MOD, d)
    outs = pl.pallas_call(
        functools.partial(_ffn1_kernel, len(later_weights)),
        grid=(n_steps,),
        in_specs=[
            pl.BlockSpec(memory_space=pltpu.SMEM),
            row_spec,
            pl.BlockSpec((step, d), lambda i: (nxt(i), 0)),
            pl.BlockSpec(mod_block, lambda i: (i // tiles_per_seq, 0, 0)),
            pl.BlockSpec(mod_block, lambda i: (nxt(i) // tiles_per_seq, 0, 0)),
            _resident((1, d)),
            _resident(wgu.shape),
            _resident(wd.shape),
        ] + cast_specs,
        out_specs=[row_spec] + cast_specs,
        out_shape=[jax.ShapeDtypeStruct((t, d), _f32)]
        + [jax.ShapeDtypeStruct(w.shape, _bf16) for w in later_weights],
        scratch_shapes=[pltpu.VMEM((step, d), _f32), pltpu.VMEM((step, d), _bf16),
                        pltpu.VMEM((step, d), _bf16)],
        compiler_params=pltpu.CompilerParams(
            dimension_semantics=("arbitrary",), vmem_limit_bytes=VMEM_LIMIT),
        name="ffn1",
    )(_runtime_zero(), h, h, mod, mod, norm_g, wgu, wd, *later_weights)
    return outs[0], outs[1:]


def _mixproj_kernel(tiles_per_seq, zero_ref, h0_ref, hn_ref, mod_ref, modn_ref, g_ref, w_ref, bm_ref,
                    cw_ref, wco_ref, q_ref, k_ref, v_ref, p1_ref, sb_ref, vbuf_ref, u_even_ref, u_odd_ref):
    i = pl.program_id(0)
    rows = ROW_TILE
    never = zero_ref[0] != 0

    @pl.when(i == 0)
    def _():
        u_even_ref[...] = _norm_mod(h0_ref[...], _gain(g_ref, mod_ref, 1), _shift(mod_ref, 1)).astype(_bf16)

    @pl.when(i % tiles_per_seq == 0)
    def _():
        vbuf_ref[0:SUBLANES, :] = jnp.zeros((SUBLANES, CONV_WIDTH), _f32)

    def step(u_ref, un_ref):
        jobs = _next_norm_jobs(never, hn_ref, un_ref, _gain(g_ref, modn_ref, 1), _shift(modn_ref, 1))
        per_dot = len(jobs) // 4
        u = u_ref[...]
        qkv0 = 3 * CONV_WIDTH
        g0 = qkv0 + 3 * ATTN_WIDTH
        pc = _dot(u, w_ref[:, 0:qkv0])
        token = _run_jobs(_token(pc), jobs[0:per_dot])
        pq = _tie(never, token, _dot(u, w_ref[:, qkv0:g0]))
        token = _run_jobs(_token(pq), jobs[per_dot:2 * per_dot])

        cb = pc[:, 0:CONV_WIDTH]
        cv = pc[:, CONV_WIDTH:2 * CONV_WIDTH] * pc[:, 2 * CONV_WIDTH:3 * CONV_WIDTH]
        vbuf_ref[SUBLANES:SUBLANES + rows, :] = cv
        y = (cw_ref[0:1, :] * vbuf_ref[SUBLANES - 2:SUBLANES - 2 + rows, :]
             + cw_ref[1:2, :] * vbuf_ref[SUBLANES - 1:SUBLANES - 1 + rows, :]
             + cw_ref[2:3, :] * cv)
        vbuf_ref[0:SUBLANES, :] = cv[rows - SUBLANES:, :]
        conv = (cb * y).astype(_bf16)

        pga = _tie(never, token, _dot(u, w_ref[:, g0:g0 + D_MODEL]))
        token = _run_jobs(_token(pga), jobs[2 * per_dot:3 * per_dot])
        q_ref[...] = (pq[:, 0:ATTN_WIDTH] * (HEAD_DIM ** -0.5)).astype(_bf16)
        k_ref[...] = pq[:, ATTN_WIDTH:2 * ATTN_WIDTH].astype(_bf16)
        v_ref[...] = pq[:, 2 * ATTN_WIDTH:3 * ATTN_WIDTH].astype(_bf16)

        pgb = _tie(never, token, _dot(u, w_ref[:, g0 + D_MODEL:g0 + 2 * D_MODEL]))
        token = _run_jobs(_token(pgb), jobs[3 * per_dot:])
        sa = jax.nn.sigmoid(pga + bm_ref[0:1, :])
        ya = _tie(never, token, _dot(conv, wco_ref[...]))
        sb_ref[...] = jax.nn.sigmoid(pgb + bm_ref[1:2, :]).astype(_bf16)
        p1_ref[...] = (sa * ya).astype(_bf16)

    @pl.when(i % 2 == 0)
    def _():
        step(u_even_ref, u_odd_ref)

    @pl.when(i % 2 == 1)
    def _():
        step(u_odd_ref, u_even_ref)


def _mixproj(h, mod, norm_g, w_mix, b_merge, conv_w, w_conv_out, seq):
    t, d = h.shape
    tiles_per_seq = seq // ROW_TILE
    n_steps = t // ROW_TILE
    nxt = lambda i: jnp.minimum(i + 1, n_steps - 1)
    row = lambda width: pl.BlockSpec((ROW_TILE, width), lambda i: (i, 0))
    mod_block = (1, N_SUBLAYERS * N_MOD, d)
    return pl.pallas_call(
        functools.partial(_mixproj_kernel, tiles_per_seq),
        grid=(n_steps,),
        in_specs=[
            pl.BlockSpec(memory_space=pltpu.SMEM),
            pl.BlockSpec((ROW_TILE, d), lambda i: (0, 0)),
            pl.BlockSpec((ROW_TILE, d), lambda i: (nxt(i), 0)),
            pl.BlockSpec(mod_block, lambda i: (i // tiles_per_seq, 0, 0)),
            pl.BlockSpec(mod_block, lambda i: (nxt(i) // tiles_per_seq, 0, 0)),
            _resident((1, d)),
            _resident(w_mix.shape),
            _resident(b_merge.shape),
            _resident(conv_w.shape),
            _resident(w_conv_out.shape),
        ],
        out_specs=[row(ATTN_WIDTH), row(ATTN_WIDTH), row(ATTN_WIDTH), row(d), row(d)],
        out_shape=[jax.ShapeDtypeStruct((t, ATTN_WIDTH), _bf16)] * 3
        + [jax.ShapeDtypeStruct((t, d), _bf16)] * 2,
        scratch_shapes=[pltpu.VMEM((SUBLANES + ROW_TILE, CONV_WIDTH), _f32),
                        pltpu.VMEM((ROW_TILE, d), _bf16), pltpu.VMEM((ROW_TILE, d), _bf16)],
        compiler_params=pltpu.CompilerParams(
            dimension_semantics=("arbitrary",), vmem_limit_bytes=VMEM_LIMIT),
        name="mixproj",
    )(_runtime_zero(), h, h, mod, mod, norm_g, w_mix, b_merge, conv_w, w_conv_out)


def _attn_kernel(q_ref, k_ref, v_ref, tri_ref, o_ref, acc_ref, rem_ref):
    iq = pl.program_id(1)
    bq = ATTN_BLOCK
    n_pairs = q_ref.shape[2] // LANES
    lane = lax.broadcasted_iota(jnp.int32, (bq, LANES), 1).astype(_f32).astype(_bf16)
    first_head = lane < HEAD_DIM

    def split_heads(x):
        zero = jnp.zeros_like(x)
        return jnp.concatenate([jnp.where(first_head, x, zero), jnp.where(first_head, zero, x)], axis=0)

    row = lax.broadcasted_iota(jnp.int32, (2 * bq, bq), 0)
    col = lax.broadcasted_iota(jnp.int32, (2 * bq, bq), 1)
    strictly_causal = col < jnp.where(row >= bq, row - bq, row)

    pairs = range(n_pairs)
    lanes = [slice(p * LANES, (p + 1) * LANES) for p in pairs]
    q2 = [split_heads(q_ref[0, :, lanes[p]]) for p in pairs]

    def scores(p, start, n_blocks):
        return lax.dot_general(q2[p], k_ref[0, pl.ds(start, n_blocks * bq), lanes[p]],
                               (((1,), (1,)), ((), ())), preferred_element_type=_f32)

    def visit(blocks, first):
        starts = [pl.multiple_of(j * bq, bq) for j, _ in blocks]
        if len(blocks) == 3:
            both = [scores(p, starts[2], 2) for p in pairs]
            zs = [[scores(p, starts[0], 1) for p in pairs],
                  [z[:, bq:] for z in both], [z[:, :bq] for z in both]]
        else:
            zs = [[scores(p, start, 1) for p in pairs] for start in starts]
        hilo = []
        for (_, diagonal), z_block in zip(blocks, zs):
            row_hilo = []
            for z in z_block:
                sp = jnp.maximum(z, 0.0) + jnp.log(1.0 + jnp.exp(-jnp.abs(z)))
                if diagonal:
                    sp = jnp.where(strictly_causal, sp, 0.0)
                row_hilo.append(jnp.concatenate(_split_bf16(sp), axis=1))
            hilo.append(row_hilo)
        css = [[_dot(x, tri_ref[...]) for x in row_hilo] for row_hilo in hilo]
        seen = [None if first else rem_ref[p] for p in pairs]
        weights = []
        for n, (_, diagonal) in enumerate(blocks):
            row_w = []
            for p in pairs:
                log_a = zs[n][p] - css[n][p][:, :bq]
                a = jnp.exp(log_a if seen[p] is None else log_a - seen[p])
                if diagonal:
                    a = jnp.where(strictly_causal, a, 0.0)
                a = a.astype(_bf16)
                row_w.append(jnp.concatenate([a[:bq], a[bq:]], axis=1))
                total = css[n][p][:, bq:]
                seen[p] = total if seen[p] is None else seen[p] + total
            weights.append(row_w)
        for p in pairs:
            out = None
            for n, start in enumerate(starts):
                term = _dot(weights[n][p], split_heads(v_ref[0, pl.ds(start, bq), lanes[p]]))
                out = term if out is None else out + term
            if first:
                acc_ref[p] = out
            else:
                acc_ref[p] += out
            rem_ref[p] = seen[p]

    fused = iq >= 2

    @pl.when(fused)
    def _():
        visit([(iq, True), (iq - 1, False), (iq - 2, False)], first=True)

    @pl.when(jnp.logical_not(fused))
    def _():
        visit([(iq, True)], first=True)

    def unfinished():
        return jnp.min(rem_ref[...]) < SKIP_LOG

    def cond(state):
        j, live = state
        return jnp.logical_and(j >= 0, live)

    def body(state):
        j, _ = state
        visit([(j, False)], first=False)
        return j - 1, unfinished()

    lax.while_loop(cond, body, (jnp.where(fused, iq - 3, iq - 1), unfinished()))
    for p in range(n_pairs):
        o_ref[0, :, p * LANES:(p + 1) * LANES] = acc_ref[p].astype(o_ref.dtype)


def _tri_matrix():
    bq = ATTN_BLOCK
    j = lax.broadcasted_iota(jnp.int32, (bq, bq), 0)
    s = lax.broadcasted_iota(jnp.int32, (bq, bq), 1)
    tri = (j >= s).astype(_bf16)
    half = jnp.concatenate([tri, jnp.ones((bq, bq), _bf16)], axis=1)
    return jnp.concatenate([half, half], axis=0)


def _attention(q, k, v):
    batch, seq, width = q.shape
    n_pairs = width // LANES
    bq = ATTN_BLOCK
    blk = pl.BlockSpec((1, bq, width), lambda b, i: (b, i, 0))
    whole = pl.BlockSpec((1, seq, width), lambda b, i: (b, 0, 0))
    return pl.pallas_call(
        _attn_kernel,
        grid=(batch, seq // bq),
        in_specs=[blk, whole, whole, _resident((2 * bq, 2 * bq))],
        out_specs=blk,
        out_shape=jax.ShapeDtypeStruct((batch, seq, width), _bf16),
        scratch_shapes=[pltpu.VMEM((n_pairs, bq, LANES), _f32),
                        pltpu.VMEM((n_pairs, 2 * bq, bq), _f32)],
        compiler_params=pltpu.CompilerParams(
            dimension_semantics=("arbitrary", "arbitrary"), vmem_limit_bytes=VMEM_LIMIT),
        name="attn",
    )(q, k, v, _tri_matrix())


def _tail_kernel(n_tiles, zero_ref, o_ref, p1_ref, sb_ref, h_ref, mod_ref, wao_ref, wo_ref, g3_ref, wgu_ref,
                 wd_ref, gf_ref, out_ref, acc_ref, h3_even_ref, h3_odd_ref):
    s = pl.program_id(0)
    never = zero_ref[0] != 0

    def final_norm_jobs(h3_ref):
        def job(k):
            rows = slice(k * PIECE_ROWS, (k + 1) * PIECE_ROWS)

            def run(token):
                x = h3_ref[rows, :]
                inv = lax.rsqrt(jnp.mean(x * x, axis=-1, keepdims=True) + EPS)
                piece = (x * inv) * _tie(never, token, gf_ref[...])
                out_ref[rows, :] = piece
                return _token(piece)
            return run
        return [job(k) for k in range(ROW_TILE // PIECE_ROWS)]

    @pl.when(s == 0)
    def _():
        h3_odd_ref[...] = jnp.zeros_like(h3_odd_ref)

    def compute(h3_ref, h3_prev_ref):
        yb = _dot(o_ref[...], wao_ref[...])
        merged = p1_ref[...].astype(_f32) + sb_ref[...].astype(_f32) * yb
        y = _dot(merged.astype(_bf16), wo_ref[...])
        h2 = h_ref[...] + _gate(mod_ref, 1) * y
        u = _norm_mod(h2, _gain(g3_ref, mod_ref, 2), _shift(mod_ref, 2)).astype(_bf16)
        tokens = _swiglu_into(acc_ref, u, wgu_ref, wd_ref, never,
                              [[job] for job in final_norm_jobs(h3_prev_ref)])
        acc = acc_ref[...]
        for token in tokens:
            acc = _tie(never, token, acc)
        h3_ref[...] = h2 + (FFN_RES_WEIGHT * _gate(mod_ref, 2)) * acc

    @pl.when(jnp.logical_and(s < n_tiles, s % 2 == 0))
    def _():
        compute(h3_even_ref, h3_odd_ref)

    @pl.when(jnp.logical_and(s < n_tiles, s % 2 == 1))
    def _():
        compute(h3_odd_ref, h3_even_ref)

    @pl.when(s == n_tiles)
    def _():
        last = h3_even_ref if (n_tiles - 1) % 2 == 0 else h3_odd_ref
        _run_jobs(jnp.zeros((1, LANES), _f32), final_norm_jobs(last))


def _tail(o, p1, sb, h, mod, w_attn_out, w_out, norm_g, wgu, wd, final_g, seq):
    t, d = h.shape
    tiles_per_seq = seq // ROW_TILE
    n_tiles = t // ROW_TILE
    cur = lambda s: jnp.minimum(s, n_tiles - 1)
    prev = lambda s: jnp.maximum(s - 1, 0)
    row = lambda width: pl.BlockSpec((ROW_TILE, width), lambda s: (cur(s), 0))
    return pl.pallas_call(
        functools.partial(_tail_kernel, n_tiles),
        grid=(n_tiles + 1,),
        in_specs=[
            pl.BlockSpec(memory_space=pltpu.SMEM),
            row(ATTN_WIDTH), row(d), row(d), row(d),
            pl.BlockSpec((1, N_SUBLAYERS * N_MOD, d), lambda s: (cur(s) // tiles_per_seq, 0, 0)),
            _resident(w_attn_out.shape),
            _resident(w_out.shape),
            _resident((1, d)),
            _resident(wgu.shape),
            _resident(wd.shape),
            _resident((1, d)),
        ],
        out_specs=pl.BlockSpec((ROW_TILE, d), lambda s: (prev(s), 0)),
        out_shape=jax.ShapeDtypeStruct((t, d), _f32),
        scratch_shapes=[pltpu.VMEM((ROW_TILE, d), _f32),
                        pltpu.VMEM((ROW_TILE, d), _f32), pltpu.VMEM((ROW_TILE, d), _f32)],
        compiler_params=pltpu.CompilerParams(
            dimension_semantics=("arbitrary",), vmem_limit_bytes=VMEM_LIMIT),
        name="tail",
    )(_runtime_zero(), o, p1, sb, h, mod, w_attn_out, w_out, norm_g, wgu, wd, final_g)


def kernel(x, c, w_ada, b_ada, norm1_g, ffn1_w_gu, ffn1_w_down, norm2_g, w_mix_in, b_merge, conv_w,
           w_conv_out, w_attn_out, w_out, norm3_g, ffn2_w_gu, ffn2_w_down, final_g):
    batch, seq, d = x.shape
    assert w_ada.shape[0] == 1
    assert seq % ROW_TILE == 0 and seq % ATTN_BLOCK == 0 and d == D_MODEL
    h = x.reshape(batch * seq, d)
    mod = _adaln(c, w_ada[0], b_ada[0]).reshape(batch, N_SUBLAYERS * N_MOD, d)
    h1, (w_mix, w_co, w_ao, w_o, w_gu2, w_d2) = _ffn1(
        h, mod, norm1_g[0].reshape(1, d), ffn1_w_gu[0].astype(_bf16), ffn1_w_down[0].astype(_bf16),
        [w_mix_in[0], w_conv_out[0], w_attn_out[0], w_out[0], ffn2_w_gu[0], ffn2_w_down[0]], seq)
    q, k, v, p1, sb = _mixproj(h1, mod, norm2_g[0].reshape(1, d), w_mix, b_merge[0], conv_w[0],
                               w_co, seq)
    o = _attention(q.reshape(batch, seq, ATTN_WIDTH), k.reshape(batch, seq, ATTN_WIDTH),
                   v.reshape(batch, seq, ATTN_WIDTH)).reshape(batch * seq, ATTN_WIDTH)
    out = _tail(o, p1, sb, h1, mod, w_ao, w_o, norm3_g[0].reshape(1, d), w_gu2, w_d2,
                final_g.reshape(1, d), seq)
    return out.reshape(batch, seq, d)
```

```python
import functools

import jax
import jax.numpy as jnp
from jax import lax
from jax.experimental import pallas as pl
from jax.experimental.pallas import tpu as pltpu

D_MODEL = 1024
N_HEADS = 8
HEAD_DIM = 64
ATTN_WIDTH = N_HEADS * HEAD_DIM
CONV_WIDTH = 512
CONV_KSIZE = 3
D_FF = 2816
N_SUBLAYERS = 3
N_MOD = 3
EPS = 1e-6
FFN_RES_WEIGHT = 0.5
MIX_IN_WIDTH = 3 * CONV_WIDTH + 3 * ATTN_WIDTH + 2 * D_MODEL

LANES = 128
SUBLANES = 8
BF16_SUBLANES = 16
MXU_DIM = 256
VMEM_LIMIT = 56 * 1024 * 1024

ROW_TILE = 512
PIECE_ROWS = 64
FF_CHUNK = MXU_DIM
N_FF_CHUNKS = D_FF // FF_CHUNK
ATTN_BLOCK = 128
Q_BLOCKS_PER_STEP = 2
SKIP_LOG = 106.0

assert D_FF % FF_CHUNK == 0

_bf16 = jnp.bfloat16
_f32 = jnp.float32


def _runtime_zero():
    return jnp.zeros((1,), jnp.int32)


def _dot(a, b):
    return jnp.dot(a, b, preferred_element_type=_f32)


def _split_bf16(x):
    hi = x.astype(_bf16)
    return hi, (x - hi.astype(_f32)).astype(_bf16)


def _resident(shape):
    zeros = (0,) * len(shape)
    return pl.BlockSpec(shape, lambda *_: zeros, pipeline_mode=pl.Buffered(1))


def _norm_mod(x, gain, shift):
    inv = lax.rsqrt(jnp.mean(x * x, axis=-1, keepdims=True) + EPS)
    return (x * inv) * gain + shift


def _gain(g_ref, mod_ref, sublayer):
    return g_ref[...] * (1.0 + mod_ref[0, N_MOD * sublayer + 1:N_MOD * sublayer + 2, :])


def _shift(mod_ref, sublayer):
    return mod_ref[0, N_MOD * sublayer:N_MOD * sublayer + 1, :]


def _gate(mod_ref, sublayer):
    return mod_ref[0, N_MOD * sublayer + 2:N_MOD * sublayer + 3, :]


def _token(value):
    return value[0:1, 0:LANES].astype(_f32)


def _tie(never, token, value):
    head = jnp.where(never, token, value[:, :LANES])
    if value.shape[1] == LANES:
        return head
    return jnp.concatenate([head, value[:, LANES:]], axis=1)


def _run_jobs(token, jobs):
    for job in jobs:
        token = job(token)
    return token


def _next_norm_jobs(never, hn_ref, un_ref, gain, shift):
    def job(k):
        rows = slice(k * PIECE_ROWS, (k + 1) * PIECE_ROWS)

        def run(token):
            piece = _norm_mod(hn_ref[rows, :], _tie(never, token, gain), shift).astype(_bf16)
            un_ref[rows, :] = piece
            return _token(piece)
        return run
    return [job(k) for k in range(ROW_TILE // PIECE_ROWS)]


def _swiglu_into(acc_ref, u_bf16, wgu_ref, wd_ref, never=None, side_jobs=()):
    pending = {}
    for c in range(N_FF_CHUNKS):
        cols = slice(c * FF_CHUNK, (c + 1) * FF_CHUNK)
        up_cols = slice(D_FF + c * FF_CHUNK, D_FF + (c + 1) * FF_CHUNK)
        g = _dot(u_bf16, wgu_ref[:, cols])
        if c - 2 in pending:
            g = _tie(never, pending.pop(c - 2), g)
        if c < len(side_jobs) and side_jobs[c]:
            pending[c] = _run_jobs(_token(g), side_jobs[c])
        up = _dot(u_bf16, wgu_ref[:, up_cols])
        act = (g * jax.nn.sigmoid(g) * up).astype(_bf16)
        down = _dot(act, wd_ref[cols, :])
        if c == 0:
            acc_ref[...] = down
        else:
            acc_ref[...] += down
    return list(pending.values())


def _adaln_kernel(c_ref, w_ref, b_ref, o_ref):
    c = c_ref[...]
    c_act = c * jax.nn.sigmoid(c)
    c_hi, c_lo = _split_bf16(c_act)
    w_hi, w_lo = _split_bf16(w_ref[...])
    o_ref[...] = _dot(c_hi, w_hi) + (_dot(c_lo, w_hi) + _dot(c_hi, w_lo)) + b_ref[...]


def _adaln(c, w_ada, b_ada):
    batch, d = c.shape
    n = w_ada.shape[1]
    bn = 1024
    return pl.pallas_call(
        _adaln_kernel,
        grid=(n // bn,),
        in_specs=[
            pl.BlockSpec((batch, d), lambda j: (0, 0)),
            pl.BlockSpec((d, bn), lambda j: (0, j)),
            pl.BlockSpec((1, bn), lambda j: (0, j)),
        ],
        out_specs=pl.BlockSpec((batch, bn), lambda j: (0, j)),
        out_shape=jax.ShapeDtypeStruct((batch, n), _f32),
        compiler_params=pltpu.CompilerParams(
            dimension_semantics=("arbitrary",), vmem_limit_bytes=VMEM_LIMIT),
        name="adaln",
    )(c, w_ada, b_ada.reshape(1, n))


def _ffn1_kernel(n_cast, zero_ref, h_ref, hn_ref, mod_ref, modn_ref, g_ref, wgu_ref, wd_ref, *rest):
    cast_in = rest[:n_cast]
    o_ref = rest[n_cast]
    cast_out = rest[n_cast + 1:2 * n_cast + 1]
    acc_ref, u_even_ref, u_odd_ref = rest[2 * n_cast + 1:]
    i = pl.program_id(0)
    never = zero_ref[0] != 0

    @pl.when(i == 0)
    def _():
        u_even_ref[...] = _norm_mod(h_ref[...], _gain(g_ref, mod_ref, 0), _shift(mod_ref, 0)).astype(_bf16)

    def cast_job(src, dst):
        def run(token):
            w = _tie(never, token, src[...]).astype(_bf16)
            dst[...] = w
            return _token(w)
        return run

    def step(u_ref, un_ref):
        norm_jobs = _next_norm_jobs(never, hn_ref, un_ref, _gain(g_ref, modn_ref, 0), _shift(modn_ref, 0))
        cast_jobs = [cast_job(src, dst) for src, dst in zip(cast_in, cast_out)]
        n_tail = N_FF_CHUNKS - len(norm_jobs)
        side_jobs = [[job] for job in norm_jobs] + [cast_jobs[r::n_tail] for r in range(n_tail)]
        tokens = _swiglu_into(acc_ref, u_ref[...], wgu_ref, wd_ref, never, side_jobs)
        acc = acc_ref[...]
        for token in tokens:
            acc = _tie(never, token, acc)
        o_ref[...] = h_ref[...] + (FFN_RES_WEIGHT * _gate(mod_ref, 0)) * acc

    @pl.when(i % 2 == 0)
    def _():
        step(u_even_ref, u_odd_ref)

    @pl.when(i % 2 == 1)
    def _():
        step(u_odd_ref, u_even_ref)


def _cast_slab_spec(shape, n_steps):
    rows, cols = shape
    span = 1
    while (rows * span) % (n_steps * BF16_SUBLANES) != 0:
        span *= 2
    return pl.BlockSpec((rows * span // n_steps, cols), lambda i: (i // span, 0))


def _ffn1(h, mod, norm_g, wgu, wd, later_weights, seq):
    t, d = h.shape
    step = ROW_TILE
    tiles_per_seq = seq // step
    n_steps = t // step
    cast_specs = [_cast_slab_spec(w.shape, n_steps) for w in later_weights]
    row_spec = pl.BlockSpec((step, d), lambda i: (i, 0))
    nxt = lambda i: jnp.minimum(i + 1, n_steps - 1)
    mod_block = (1, N_SUBLAYERS * N_MOD, d)
    outs = pl.pallas_call(
        functools.partial(_ffn1_kernel, len(later_weights)),
        grid=(n_steps,),
        in_specs=[
            pl.BlockSpec(memory_space=pltpu.SMEM),
            row_spec,
            pl.BlockSpec((step, d), lambda i: (nxt(i), 0)),
            pl.BlockSpec(mod_block, lambda i: (i // tiles_per_seq, 0, 0)),
            pl.BlockSpec(mod_block, lambda i: (nxt(i) // tiles_per_seq, 0, 0)),
            _resident((1, d)),
            _resident(wgu.shape),
            _resident(wd.shape),
        ] + cast_specs,
        out_specs=[row_spec] + cast_specs,
        out_shape=[jax.ShapeDtypeStruct((t, d), _f32)]
        + [jax.ShapeDtypeStruct(w.shape, _bf16) for w in later_weights],
        scratch_shapes=[pltpu.VMEM((step, d), _f32), pltpu.VMEM((step, d), _bf16),
                        pltpu.VMEM((step, d), _bf16)],
        compiler_params=pltpu.CompilerParams(
            dimension_semantics=("arbitrary",), vmem_limit_bytes=VMEM_LIMIT),
        name="ffn1",
    )(_runtime_zero(), h, h, mod, mod, norm_g, wgu, wd, *later_weights)
    return outs[0], outs[1:]


def _mixproj_kernel(tiles_per_seq, zero_ref, h0_ref, hn_ref, mod_ref, modn_ref, g_ref, w_ref, bm_ref,
                    cw_ref, wco_ref, q_ref, k_ref, v_ref, p1_ref, sb_ref, vbuf_ref, u_even_ref, u_odd_ref):
    i = pl.program_id(0)
    rows = ROW_TILE
    never = zero_ref[0] != 0

    @pl.when(i == 0)
    def _():
        u_even_ref[...] = _norm_mod(h0_ref[...], _gain(g_ref, mod_ref, 1), _shift(mod_ref, 1)).astype(_bf16)

    @pl.when(i % tiles_per_seq == 0)
    def _():
        vbuf_ref[0:SUBLANES, :] = jnp.zeros((SUBLANES, CONV_WIDTH), _f32)

    def step(u_ref, un_ref):
        jobs = _next_norm_jobs(never, hn_ref, un_ref, _gain(g_ref, modn_ref, 1), _shift(modn_ref, 1))
        per_dot = len(jobs) // 4
        u = u_ref[...]
        qkv0 = 3 * CONV_WIDTH
        g0 = qkv0 + 3 * ATTN_WIDTH
        pc = _dot(u, w_ref[:, 0:qkv0])
        token = _run_jobs(_token(pc), jobs[0:per_dot])
        pq = _tie(never, token, _dot(u, w_ref[:, qkv0:g0]))
        token = _run_jobs(_token(pq), jobs[per_dot:2 * per_dot])

        cb = pc[:, 0:CONV_WIDTH]
        cv = pc[:, CONV_WIDTH:2 * CONV_WIDTH] * pc[:, 2 * CONV_WIDTH:3 * CONV_WIDTH]
        vbuf_ref[SUBLANES:SUBLANES + rows, :] = cv
        y = (cw_ref[0:1, :] * vbuf_ref[SUBLANES - 2:SUBLANES - 2 + rows, :]
             + cw_ref[1:2, :] * vbuf_ref[SUBLANES - 1:SUBLANES - 1 + rows, :]
             + cw_ref[2:3, :] * cv)
        vbuf_ref[0:SUBLANES, :] = cv[rows - SUBLANES:, :]
        conv = (cb * y).astype(_bf16)

        pga = _tie(never, token, _dot(u, w_ref[:, g0:g0 + D_MODEL]))
        token = _run_jobs(_token(pga), jobs[2 * per_dot:3 * per_dot])
        q_ref[...] = (pq[:, 0:ATTN_WIDTH] * (HEAD_DIM ** -0.5)).astype(_bf16)
        k_ref[...] = pq[:, ATTN_WIDTH:2 * ATTN_WIDTH].astype(_bf16)
        v_ref[...] = pq[:, 2 * ATTN_WIDTH:3 * ATTN_WIDTH].astype(_bf16)

        pgb = _tie(never, token, _dot(u, w_ref[:, g0 + D_MODEL:g0 + 2 * D_MODEL]))
        token = _run_jobs(_token(pgb), jobs[3 * per_dot:])
        sa = jax.nn.sigmoid(pga + bm_ref[0:1, :])
        ya = _tie(never, token, _dot(conv, wco_ref[...]))
        sb_ref[...] = jax.nn.sigmoid(pgb + bm_ref[1:2, :]).astype(_bf16)
        p1_ref[...] = (sa * ya).astype(_bf16)

    @pl.when(i % 2 == 0)
    def _():
        step(u_even_ref, u_odd_ref)

    @pl.when(i % 2 == 1)
    def _():
        step(u_odd_ref, u_even_ref)


def _mixproj(h, mod, norm_g, w_mix, b_merge, conv_w, w_conv_out, seq):
    t, d = h.shape
    tiles_per_seq = seq // ROW_TILE
    n_steps = t // ROW_TILE
    nxt = lambda i: jnp.minimum(i + 1, n_steps - 1)
    row = lambda width: pl.BlockSpec((ROW_TILE, width), lambda i: (i, 0))
    mod_block = (1, N_SUBLAYERS * N_MOD, d)
    return pl.pallas_call(
        functools.partial(_mixproj_kernel, tiles_per_seq),
        grid=(n_steps,),
        in_specs=[
            pl.BlockSpec(memory_space=pltpu.SMEM),
            pl.BlockSpec((ROW_TILE, d), lambda i: (0, 0)),
            pl.BlockSpec((ROW_TILE, d), lambda i: (nxt(i), 0)),
            pl.BlockSpec(mod_block, lambda i: (i // tiles_per_seq, 0, 0)),
            pl.BlockSpec(mod_block, lambda i: (nxt(i) // tiles_per_seq, 0, 0)),
            _resident((1, d)),
            _resident(w_mix.shape),
            _resident(b_merge.shape),
            _resident(conv_w.shape),
            _resident(w_conv_out.shape),
        ],
        out_specs=[row(ATTN_WIDTH), row(ATTN_WIDTH), row(ATTN_WIDTH), row(d), row(d)],
        out_shape=[jax.ShapeDtypeStruct((t, ATTN_WIDTH), _bf16)] * 3
        + [jax.ShapeDtypeStruct((t, d), _bf16)] * 2,
        scratch_shapes=[pltpu.VMEM((SUBLANES + ROW_TILE, CONV_WIDTH), _f32),
                        pltpu.VMEM((ROW_TILE, d), _bf16), pltpu.VMEM((ROW_TILE, d), _bf16)],
        compiler_params=pltpu.CompilerParams(
            dimension_semantics=("arbitrary",), vmem_limit_bytes=VMEM_LIMIT),
        name="mixproj",
    )(_runtime_zero(), h, h, mod, mod, norm_g, w_mix, b_merge, conv_w, w_conv_out)


def _attn_kernel(q_ref, k_ref, v_ref, tri_ref, o_ref, acc_ref, rem_ref):
    step = pl.program_id(1)
    bq = ATTN_BLOCK
    n_pairs = q_ref.shape[2] // LANES
    lane = lax.broadcasted_iota(jnp.int32, (bq, LANES), 1).astype(_f32).astype(_bf16)
    first_head = lane < HEAD_DIM

    def split_heads(x):
        zero = jnp.zeros_like(x)
        return jnp.concatenate([jnp.where(first_head, x, zero), jnp.where(first_head, zero, x)], axis=0)

    row = lax.broadcasted_iota(jnp.int32, (2 * bq, bq), 0)
    col = lax.broadcasted_iota(jnp.int32, (2 * bq, bq), 1)
    strictly_causal = col < jnp.where(row >= bq, row - bq, row)

    pairs = range(n_pairs)
    lanes = [slice(p * LANES, (p + 1) * LANES) for p in pairs]
    q_rows = [slice(b * bq, (b + 1) * bq) for b in range(Q_BLOCKS_PER_STEP)]
    q2 = [[split_heads(q_ref[0, q_rows[b], lanes[p]]) for p in pairs] for b in range(Q_BLOCKS_PER_STEP)]

    def scores(b, p, start, n_blocks):
        return lax.dot_general(q2[b][p], k_ref[0, pl.ds(start, n_blocks * bq), lanes[p]],
                               (((1,), (1,)), ((), ())), preferred_element_type=_f32)

    def visit(plan, first):
        chains = []
        for b, blocks in plan:
            starts = [pl.multiple_of(j * bq, bq) for j, _ in blocks]
            if len(blocks) == 3:
                both = [scores(b, p, starts[2], 2) for p in pairs]
                zs = [[scores(b, p, starts[0], 1) for p in pairs],
                      [z[:, bq:] for z in both], [z[:, :bq] for z in both]]
            else:
                zs = [[scores(b, p, start, 1) for p in pairs] for start in starts]
            chains += [(b, start, diagonal, z) for start, (_, diagonal), z in zip(starts, blocks, zs)]
        sps = []
        for _, _, diagonal, z_block in chains:
            row_sp = []
            for z in z_block:
                sp = jnp.maximum(z, 0.0) + jnp.log(1.0 + jnp.exp(-jnp.abs(z)))
                if diagonal:
                    sp = jnp.where(strictly_causal, sp, 0.0)
                row_sp.append(sp.astype(_bf16))
            sps.append(row_sp)
        css = [[_dot(x, tri_ref[...]) for x in row_sp] for row_sp in sps]
        seen = {b: [None if first else rem_ref[b, p] for p in pairs] for b, _ in plan}
        weights = []
        for n, (b, _, diagonal, z_block) in enumerate(chains):
            row_w = []
            for p in pairs:
                log_a = z_block[p] - css[n][p][:, :bq]
                a = jnp.exp(log_a if seen[b][p] is None else log_a - seen[b][p])
                if diagonal:
                    a = jnp.where(strictly_causal, a, 0.0)
                a = a.astype(_bf16)
                row_w.append(jnp.concatenate([a[:bq], a[bq:]], axis=1))
                total = css[n][p][:, bq:]
                seen[b][p] = total if seen[b][p] is None else seen[b][p] + total
            weights.append(row_w)
        for b, _ in plan:
            for p in pairs:
                out = None
                for n, (cb, start, _, _) in enumerate(chains):
                    if cb != b:
                        continue
                    term = _dot(weights[n][p], split_heads(v_ref[0, pl.ds(start, bq), lanes[p]]))
                    out = term if out is None else out + term
                if first:
                    acc_ref[b, p] = out
                else:
                    acc_ref[b, p] += out
                rem_ref[b, p] = seen[b][p]

    first_block = step * Q_BLOCKS_PER_STEP
    fused = step >= 1

    @pl.when(fused)
    def _():
        visit([(b, [(first_block + b, True), (first_block + b - 1, False), (first_block + b - 2, False)])
               for b in reversed(range(Q_BLOCKS_PER_STEP))], first=True)

    @pl.when(jnp.logical_not(fused))
    def _():
        visit([(b, [(b, True)] + [(j, False) for j in reversed(range(b))])
               for b in reversed(range(Q_BLOCKS_PER_STEP))], first=True)

    for b in range(Q_BLOCKS_PER_STEP):
        def unfinished(b=b):
            return jnp.min(rem_ref[b]) < SKIP_LOG

        def cond(state):
            j, live = state
            return jnp.logical_and(j >= 0, live)

        def body(state, b=b, unfinished=unfinished):
            j, _ = state
            visit([(b, [(j, False)])], first=False)
            return j - 1, unfinished()

        lax.while_loop(cond, body, (jnp.where(fused, first_block + b - 3, -1), unfinished()))
    for b in range(Q_BLOCKS_PER_STEP):
        for p in pairs:
            o_ref[0, q_rows[b], lanes[p]] = acc_ref[b, p].astype(o_ref.dtype)


def _tri_matrix():
    bq = ATTN_BLOCK
    j = lax.broadcasted_iota(jnp.int32, (bq, bq), 0)
    s = lax.broadcasted_iota(jnp.int32, (bq, bq), 1)
    return jnp.concatenate([(j >= s).astype(_bf16), jnp.ones((bq, bq), _bf16)], axis=1)


def _attention(q, k, v):
    batch, seq, width = q.shape
    n_pairs = width // LANES
    bq = ATTN_BLOCK
    rows = Q_BLOCKS_PER_STEP * bq
    blk = pl.BlockSpec((1, rows, width), lambda b, i: (b, i, 0))
    whole = pl.BlockSpec((1, seq, width), lambda b, i: (b, 0, 0))
    return pl.pallas_call(
        _attn_kernel,
        grid=(batch, seq // rows),
        in_specs=[blk, whole, whole, _resident((bq, 2 * bq))],
        out_specs=blk,
        out_shape=jax.ShapeDtypeStruct((batch, seq, width), _bf16),
        scratch_shapes=[pltpu.VMEM((Q_BLOCKS_PER_STEP, n_pairs, bq, LANES), _f32),
                        pltpu.VMEM((Q_BLOCKS_PER_STEP, n_pairs, 2 * bq, bq), _f32)],
        compiler_params=pltpu.CompilerParams(
            dimension_semantics=("arbitrary", "arbitrary"), vmem_limit_bytes=VMEM_LIMIT),
        name="attn",
    )(q, k, v, _tri_matrix())


def _tail_kernel(o_ref, p1_ref, sb_ref, h_ref, mod_ref, wao_ref, wo_ref, g3_ref, wgu_ref, wd_ref,
                 gf_ref, out_ref, acc_ref):
    yb = _dot(o_ref[...], wao_ref[...])
    merged = p1_ref[...].astype(_f32) + sb_ref[...].astype(_f32) * yb
    y = _dot(merged.astype(_bf16), wo_ref[...])
    h2 = h_ref[...] + _gate(mod_ref, 1) * y
    u = _norm_mod(h2, _gain(g3_ref, mod_ref, 2), _shift(mod_ref, 2)).astype(_bf16)
    _swiglu_into(acc_ref, u, wgu_ref, wd_ref)
    h3 = h2 + (FFN_RES_WEIGHT * _gate(mod_ref, 2)) * acc_ref[...]
    inv = lax.rsqrt(jnp.mean(h3 * h3, axis=-1, keepdims=True) + EPS)
    out_ref[...] = (h3 * inv) * gf_ref[...]


def _tail(o, p1, sb, h, mod, w_attn_out, w_out, norm_g, wgu, wd, final_g, seq):
    t, d = h.shape
    tiles_per_seq = seq // ROW_TILE
    row = lambda width: pl.BlockSpec((ROW_TILE, width), lambda i: (i, 0))
    return pl.pallas_call(
        _tail_kernel,
        grid=(t // ROW_TILE,),
        in_specs=[
            row(ATTN_WIDTH), row(d), row(d), row(d),
            pl.BlockSpec((1, N_SUBLAYERS * N_MOD, d), lambda i: (i // tiles_per_seq, 0, 0)),
            _resident(w_attn_out.shape),
            _resident(w_out.shape),
            _resident((1, d)),
            _resident(wgu.shape),
            _resident(wd.shape),
            _resident((1, d)),
        ],
        out_specs=row(d),
        out_shape=jax.ShapeDtypeStruct((t, d), _f32),
        scratch_shapes=[pltpu.VMEM((ROW_TILE, d), _f32)],
        compiler_params=pltpu.CompilerParams(
            dimension_semantics=("arbitrary",), vmem_limit_bytes=VMEM_LIMIT),
        name="tail",
    )(o, p1, sb, h, mod, w_attn_out, w_out, norm_g, wgu, wd, final_g)


def kernel(x, c, w_ada, b_ada, norm1_g, ffn1_w_gu, ffn1_w_down, norm2_g, w_mix_in, b_merge, conv_w,
           w_conv_out, w_attn_out, w_out, norm3_g, ffn2_w_gu, ffn2_w_down, final_g):
    batch, seq, d = x.shape
    assert w_ada.shape[0] == 1
    assert seq % ROW_TILE == 0 and seq % (Q_BLOCKS_PER_STEP * ATTN_BLOCK) == 0 and d == D_MODEL
    h = x.reshape(batch * seq, d)
    mod = _adaln(c, w_ada[0], b_ada[0]).reshape(batch, N_SUBLAYERS * N_MOD, d)
    h1, (w_mix, w_co, w_ao, w_o, w_gu2, w_d2) = _ffn1(
        h, mod, norm1_g[0].reshape(1, d), ffn1_w_gu[0].astype(_bf16), ffn1_w_down[0].astype(_bf16),
        [w_mix_in[0], w_conv_out[0], w_attn_out[0], w_out[0], ffn2_w_gu[0], ffn2_w_down[0]], seq)
    q, k, v, p1, sb = _mixproj(h1, mod, norm2_g[0].reshape(1, d), w_mix, b_merge[0], conv_w[0],
                               w_co, seq)
    o = _attention(q.reshape(batch, seq, ATTN_WIDTH), k.reshape(batch, seq, ATTN_WIDTH),
                   v.reshape(batch, seq, ATTN_WIDTH)).reshape(batch * seq, ATTN_WIDTH)
    out = _tail(o, p1, sb, h1, mod, w_ao, w_o, norm3_g[0].reshape(1, d), w_gu2, w_d2,
                final_g.reshape(1, d), seq)
    return out.reshape(batch, seq, d)
```

```python
import functools

import jax
import jax.numpy as jnp
from jax import lax
from jax.experimental import pallas as pl
from jax.experimental.pallas import tpu as pltpu

D_MODEL = 1024
N_HEADS = 8
HEAD_DIM = 64
ATTN_WIDTH = N_HEADS * HEAD_DIM
CONV_WIDTH = 512
CONV_KSIZE = 3
D_FF = 2816
N_SUBLAYERS = 3
N_MOD = 3
EPS = 1e-6
FFN_RES_WEIGHT = 0.5
MIX_IN_WIDTH = 3 * CONV_WIDTH + 3 * ATTN_WIDTH + 2 * D_MODEL

LANES = 128
SUBLANES = 8
BF16_SUBLANES = 16
MXU_DIM = 256
VMEM_LIMIT = 56 * 1024 * 1024

ADALN_COLS = 3072
ROW_TILE = 512
PIECE_ROWS = 64
FF_CHUNK = MXU_DIM
N_FF_CHUNKS = D_FF // FF_CHUNK
ATTN_BLOCK = 128
Q_BLOCKS_PER_STEP = 4
SKIP_LOG = 106.0

assert D_FF % FF_CHUNK == 0

_bf16 = jnp.bfloat16
_f32 = jnp.float32


def _runtime_zero():
    return jnp.zeros((1,), jnp.int32)


def _dot(a, b):
    return jnp.dot(a, b, preferred_element_type=_f32)


def _split_bf16(x):
    hi = x.astype(_bf16)
    return hi, (x - hi.astype(_f32)).astype(_bf16)


def _resident(shape):
    zeros = (0,) * len(shape)
    return pl.BlockSpec(shape, lambda *_: zeros, pipeline_mode=pl.Buffered(1))


def _norm_mod(x, gain, shift):
    inv = lax.rsqrt(jnp.mean(x * x, axis=-1, keepdims=True) + EPS)
    return (x * inv) * gain + shift


def _gain(g_ref, mod_ref, sublayer):
    return g_ref[...] * (1.0 + mod_ref[0, N_MOD * sublayer + 1:N_MOD * sublayer + 2, :])


def _shift(mod_ref, sublayer):
    return mod_ref[0, N_MOD * sublayer:N_MOD * sublayer + 1, :]


def _gate(mod_ref, sublayer):
    return mod_ref[0, N_MOD * sublayer + 2:N_MOD * sublayer + 3, :]


def _token(value):
    return value[0:1, 0:LANES].astype(_f32)


def _tie(never, token, value):
    head = jnp.where(never, token, value[:, :LANES])
    if value.shape[1] == LANES:
        return head
    return jnp.concatenate([head, value[:, LANES:]], axis=1)


def _run_jobs(token, jobs):
    for job in jobs:
        token = job(token)
    return token


def _next_norm_jobs(never, hn_ref, un_ref, gain, shift):
    def job(k):
        rows = slice(k * PIECE_ROWS, (k + 1) * PIECE_ROWS)

        def run(token):
            piece = _norm_mod(hn_ref[rows, :], _tie(never, token, gain), shift).astype(_bf16)
            un_ref[rows, :] = piece
            return _token(piece)
        return run
    return [job(k) for k in range(ROW_TILE // PIECE_ROWS)]


def _swiglu_into(acc_ref, u_bf16, wgu_ref, wd_ref, never=None, side_jobs=()):
    pending = {}
    for c in range(N_FF_CHUNKS):
        cols = slice(c * FF_CHUNK, (c + 1) * FF_CHUNK)
        up_cols = slice(D_FF + c * FF_CHUNK, D_FF + (c + 1) * FF_CHUNK)
        g = _dot(u_bf16, wgu_ref[:, cols])
        if c - 2 in pending:
            g = _tie(never, pending.pop(c - 2), g)
        if c < len(side_jobs) and side_jobs[c]:
            pending[c] = _run_jobs(_token(g), side_jobs[c])
        up = _dot(u_bf16, wgu_ref[:, up_cols])
        act = (g * jax.nn.sigmoid(g) * up).astype(_bf16)
        down = _dot(act, wd_ref[cols, :])
        if c == 0:
            acc_ref[...] = down
        else:
            acc_ref[...] += down
    return list(pending.values())


def _adaln_kernel(c_ref, w_ref, b_ref, o_ref):
    c = c_ref[...]
    c_act = c * jax.nn.sigmoid(c)
    c_hi, c_lo = _split_bf16(c_act)
    w_hi, w_lo = _split_bf16(w_ref[...])
    o_ref[...] = _dot(c_hi, w_hi) + (_dot(c_lo, w_hi) + _dot(c_hi, w_lo)) + b_ref[...]


def _adaln(c, w_ada, b_ada):
    batch, d = c.shape
    n = w_ada.shape[1]
    bn = ADALN_COLS
    return pl.pallas_call(
        _adaln_kernel,
        grid=(n // bn,),
        in_specs=[
            pl.BlockSpec((batch, d), lambda j: (0, 0)),
            pl.BlockSpec((d, bn), lambda j: (0, j)),
            pl.BlockSpec((1, bn), lambda j: (0, j)),
        ],
        out_specs=pl.BlockSpec((batch, bn), lambda j: (0, j)),
        out_shape=jax.ShapeDtypeStruct((batch, n), _f32),
        compiler_params=pltpu.CompilerParams(
            dimension_semantics=("arbitrary",), vmem_limit_bytes=VMEM_LIMIT),
        name="adaln",
    )(c, w_ada, b_ada.reshape(1, n))


def _ffn1_kernel(n_cast, zero_ref, h_ref, hn_ref, mod_ref, modn_ref, g_ref, wgu_ref, wd_ref, *rest):
    cast_in = rest[:n_cast]
    o_ref = rest[n_cast]
    cast_out = rest[n_cast + 1:2 * n_cast + 1]
    acc_ref, u_even_ref, u_odd_ref = rest[2 * n_cast + 1:]
    i = pl.program_id(0)
    never = zero_ref[0] != 0

    @pl.when(i == 0)
    def _():
        u_even_ref[...] = _norm_mod(h_ref[...], _gain(g_ref, mod_ref, 0), _shift(mod_ref, 0)).astype(_bf16)

    def cast_job(src, dst):
        def run(token):
            w = _tie(never, token, src[...]).astype(_bf16)
            dst[...] = w
            return _token(w)
        return run

    def step(u_ref, un_ref):
        norm_jobs = _next_norm_jobs(never, hn_ref, un_ref, _gain(g_ref, modn_ref, 0), _shift(modn_ref, 0))
        cast_jobs = [cast_job(src, dst) for src, dst in zip(cast_in, cast_out)]
        n_tail = N_FF_CHUNKS - len(norm_jobs)
        side_jobs = [[job] for job in norm_jobs] + [cast_jobs[r::n_tail] for r in range(n_tail)]
        tokens = _swiglu_into(acc_ref, u_ref[...], wgu_ref, wd_ref, never, side_jobs)
        acc = acc_ref[...]
        for token in tokens:
            acc = _tie(never, token, acc)
        o_ref[...] = h_ref[...] + (FFN_RES_WEIGHT * _gate(mod_ref, 0)) * acc

    @pl.when(i % 2 == 0)
    def _():
        step(u_even_ref, u_odd_ref)

    @pl.when(i % 2 == 1)
    def _():
        step(u_odd_ref, u_even_ref)


def _cast_slab_spec(shape, n_steps):
    rows, cols = shape
    span = 1
    while (rows * span) % (n_steps * BF16_SUBLANES) != 0:
        span *= 2
    return pl.BlockSpec((rows * span // n_steps, cols), lambda i: (i // span, 0))


def _ffn1(h, mod, norm_g, wgu, wd, later_weights, seq):
    t, d = h.shape
    step = ROW_TILE
    tiles_per_seq = seq // step
    n_steps = t // step
    cast_specs = [_cast_slab_spec(w.shape, n_steps) for w in later_weights]
    row_spec = pl.BlockSpec((step, d), lambda i: (i, 0))
    nxt = lambda i: jnp.minimum(i + 1, n_steps - 1)
    mod_block = (1, N_SUBLAYERS * N_MOD, d)
    outs = pl.pallas_call(
        functools.partial(_ffn1_kernel, len(later_weights)),
        grid=(n_steps,),
        in_specs=[
            pl.BlockSpec(memory_space=pltpu.SMEM),
            row_spec,
            pl.BlockSpec((step, d), lambda i: (nxt(i), 0)),
            pl.BlockSpec(mod_block, lambda i: (i // tiles_per_seq, 0, 0)),
            pl.BlockSpec(mod_block, lambda i: (nxt(i) // tiles_per_seq, 0, 0)),
            _resident((1, d)),
            _resident(wgu.shape),
            _resident(wd.shape),
        ] + cast_specs,
        out_specs=[row_spec] + cast_specs,
        out_shape=[jax.ShapeDtypeStruct((t, d), _f32)]
        + [jax.ShapeDtypeStruct(w.shape, _bf16) for w in later_weights],
        scratch_shapes=[pltpu.VMEM((step, d), _f32), pltpu.VMEM((step, d), _bf16),
                        pltpu.VMEM((step, d), _bf16)],
        compiler_params=pltpu.CompilerParams(
            dimension_semantics=("arbitrary",), vmem_limit_bytes=VMEM_LIMIT),
        name="ffn1",
    )(_runtime_zero(), h, h, mod, mod, norm_g, wgu, wd, *later_weights)
    return outs[0], outs[1:]


def _mixproj_kernel(tiles_per_seq, zero_ref, h0_ref, hn_ref, mod_ref, modn_ref, g_ref, w_ref, bm_ref,
                    cw_ref, wco_ref, q_ref, k_ref, v_ref, p1_ref, sb_ref, vbuf_ref, u_even_ref, u_odd_ref):
    i = pl.program_id(0)
    rows = ROW_TILE
    never = zero_ref[0] != 0

    @pl.when(i == 0)
    def _():
        u_even_ref[...] = _norm_mod(h0_ref[...], _gain(g_ref, mod_ref, 1), _shift(mod_ref, 1)).astype(_bf16)

    @pl.when(i % tiles_per_seq == 0)
    def _():
        vbuf_ref[0:SUBLANES, :] = jnp.zeros((SUBLANES, CONV_WIDTH), _f32)

    def step(u_ref, un_ref):
        jobs = _next_norm_jobs(never, hn_ref, un_ref, _gain(g_ref, modn_ref, 1), _shift(modn_ref, 1))
        per_dot = len(jobs) // 4
        u = u_ref[...]
        qkv0 = 3 * CONV_WIDTH
        g0 = qkv0 + 3 * ATTN_WIDTH
        pc = _dot(u, w_ref[:, 0:qkv0])
        token = _run_jobs(_token(pc), jobs[0:per_dot])
        pq = _tie(never, token, _dot(u, w_ref[:, qkv0:g0]))
        token = _run_jobs(_token(pq), jobs[per_dot:2 * per_dot])

        cb = pc[:, 0:CONV_WIDTH]
        cv = pc[:, CONV_WIDTH:2 * CONV_WIDTH] * pc[:, 2 * CONV_WIDTH:3 * CONV_WIDTH]
        vbuf_ref[SUBLANES:SUBLANES + rows, :] = cv
        y = (cw_ref[0:1, :] * vbuf_ref[SUBLANES - 2:SUBLANES - 2 + rows, :]
             + cw_ref[1:2, :] * vbuf_ref[SUBLANES - 1:SUBLANES - 1 + rows, :]
             + cw_ref[2:3, :] * cv)
        vbuf_ref[0:SUBLANES, :] = cv[rows - SUBLANES:, :]
        conv = (cb * y).astype(_bf16)

        pga = _tie(never, token, _dot(u, w_ref[:, g0:g0 + D_MODEL]))
        token = _run_jobs(_token(pga), jobs[2 * per_dot:3 * per_dot])
        q_ref[...] = (pq[:, 0:ATTN_WIDTH] * (HEAD_DIM ** -0.5)).astype(_bf16)
        k_ref[...] = pq[:, ATTN_WIDTH:2 * ATTN_WIDTH].astype(_bf16)
        v_ref[...] = pq[:, 2 * ATTN_WIDTH:3 * ATTN_WIDTH].astype(_bf16)

        pgb = _tie(never, token, _dot(u, w_ref[:, g0 + D_MODEL:g0 + 2 * D_MODEL]))
        token = _run_jobs(_token(pgb), jobs[3 * per_dot:])
        sa = jax.nn.sigmoid(pga + bm_ref[0:1, :])
        ya = _tie(never, token, _dot(conv, wco_ref[...]))
        sb_ref[...] = jax.nn.sigmoid(pgb + bm_ref[1:2, :]).astype(_bf16)
        p1_ref[...] = (sa * ya).astype(_bf16)

    @pl.when(i % 2 == 0)
    def _():
        step(u_even_ref, u_odd_ref)

    @pl.when(i % 2 == 1)
    def _():
        step(u_odd_ref, u_even_ref)


def _mixproj(h, mod, norm_g, w_mix, b_merge, conv_w, w_conv_out, seq):
    t, d = h.shape
    tiles_per_seq = seq // ROW_TILE
    n_steps = t // ROW_TILE
    nxt = lambda i: jnp.minimum(i + 1, n_steps - 1)
    row = lambda width: pl.BlockSpec((ROW_TILE, width), lambda i: (i, 0))
    mod_block = (1, N_SUBLAYERS * N_MOD, d)
    return pl.pallas_call(
        functools.partial(_mixproj_kernel, tiles_per_seq),
        grid=(n_steps,),
        in_specs=[
            pl.BlockSpec(memory_space=pltpu.SMEM),
            pl.BlockSpec((ROW_TILE, d), lambda i: (0, 0)),
            pl.BlockSpec((ROW_TILE, d), lambda i: (nxt(i), 0)),
            pl.BlockSpec(mod_block, lambda i: (i // tiles_per_seq, 0, 0)),
            pl.BlockSpec(mod_block, lambda i: (nxt(i) // tiles_per_seq, 0, 0)),
            _resident((1, d)),
            _resident(w_mix.shape),
            _resident(b_merge.shape),
            _resident(conv_w.shape),
            _resident(w_conv_out.shape),
        ],
        out_specs=[row(ATTN_WIDTH), row(ATTN_WIDTH), row(ATTN_WIDTH), row(d), row(d)],
        out_shape=[jax.ShapeDtypeStruct((t, ATTN_WIDTH), _bf16)] * 3
        + [jax.ShapeDtypeStruct((t, d), _bf16)] * 2,
        scratch_shapes=[pltpu.VMEM((SUBLANES + ROW_TILE, CONV_WIDTH), _f32),
                        pltpu.VMEM((ROW_TILE, d), _bf16), pltpu.VMEM((ROW_TILE, d), _bf16)],
        compiler_params=pltpu.CompilerParams(
            dimension_semantics=("arbitrary",), vmem_limit_bytes=VMEM_LIMIT),
        name="mixproj",
    )(_runtime_zero(), h, h, mod, mod, norm_g, w_mix, b_merge, conv_w, w_conv_out)


def _attn_kernel(q_ref, k_ref, v_ref, tri_ref, o_ref, acc_ref, rem_ref):
    step = pl.program_id(1)
    bq = ATTN_BLOCK
    n_pairs = q_ref.shape[2] // LANES
    lane = lax.broadcasted_iota(jnp.int32, (bq, LANES), 1).astype(_f32).astype(_bf16)
    first_head = lane < HEAD_DIM

    def split_heads(x):
        zero = jnp.zeros_like(x)
        return jnp.concatenate([jnp.where(first_head, x, zero), jnp.where(first_head, zero, x)], axis=0)

    row = lax.broadcasted_iota(jnp.int32, (2 * bq, bq), 0)
    col = lax.broadcasted_iota(jnp.int32, (2 * bq, bq), 1)
    strictly_causal = col < jnp.where(row >= bq, row - bq, row)

    pairs = range(n_pairs)
    lanes = [slice(p * LANES, (p + 1) * LANES) for p in pairs]
    q_rows = [slice(b * bq, (b + 1) * bq) for b in range(Q_BLOCKS_PER_STEP)]
    q2 = [[split_heads(q_ref[0, q_rows[b], lanes[p]]) for p in pairs] for b in range(Q_BLOCKS_PER_STEP)]

    def scores(b, p, start, n_blocks):
        return lax.dot_general(q2[b][p], k_ref[0, pl.ds(start, n_blocks * bq), lanes[p]],
                               (((1,), (1,)), ((), ())), preferred_element_type=_f32)

    def visit(plan, first):
        chains = []
        for b, blocks in plan:
            starts = [pl.multiple_of(j * bq, bq) for j, _ in blocks]
            if len(blocks) == 3:
                both = [scores(b, p, starts[2], 2) for p in pairs]
                zs = [[scores(b, p, starts[0], 1) for p in pairs],
                      [z[:, bq:] for z in both], [z[:, :bq] for z in both]]
            else:
                zs = [[scores(b, p, start, 1) for p in pairs] for start in starts]
            chains += [(b, start, diagonal, z) for start, (_, diagonal), z in zip(starts, blocks, zs)]
        sps = []
        for _, _, diagonal, z_block in chains:
            row_sp = []
            for z in z_block:
                sp = jnp.maximum(z, 0.0) + jnp.log(1.0 + jnp.exp(-jnp.abs(z)))
                if diagonal:
                    sp = jnp.where(strictly_causal, sp, 0.0)
                row_sp.append(sp.astype(_bf16))
            sps.append(row_sp)
        css = [[_dot(x, tri_ref[...]) for x in row_sp] for row_sp in sps]
        seen = {b: [None if first else rem_ref[b, p] for p in pairs] for b, _ in plan}
        weights = []
        for n, (b, _, diagonal, z_block) in enumerate(chains):
            row_w = []
            for p in pairs:
                log_a = z_block[p] - css[n][p][:, :bq]
                a = jnp.exp(log_a if seen[b][p] is None else log_a - seen[b][p])
                if diagonal:
                    a = jnp.where(strictly_causal, a, 0.0)
                a = a.astype(_bf16)
                row_w.append(jnp.concatenate([a[:bq], a[bq:]], axis=1))
                total = css[n][p][:, bq:]
                seen[b][p] = total if seen[b][p] is None else seen[b][p] + total
            weights.append(row_w)
        for b, _ in plan:
            for p in pairs:
                out = None
                for n, (cb, start, _, _) in enumerate(chains):
                    if cb != b:
                        continue
                    term = _dot(weights[n][p], split_heads(v_ref[0, pl.ds(start, bq), lanes[p]]))
                    out = term if out is None else out + term
                if first:
                    acc_ref[b, p] = out
                else:
                    acc_ref[b, p] += out
                rem_ref[b, p] = seen[b][p]

    first_block = step * Q_BLOCKS_PER_STEP
    fused = step >= 1

    @pl.when(fused)
    def _():
        visit([(b, [(first_block + b, True), (first_block + b - 1, False), (first_block + b - 2, False)])
               for b in reversed(range(Q_BLOCKS_PER_STEP))], first=True)

    @pl.when(jnp.logical_not(fused))
    def _():
        visit([(b, [(b, True)] + [(j, False) for j in reversed(range(b))])
               for b in reversed(range(Q_BLOCKS_PER_STEP))], first=True)

    def unfinished(b):
        return jnp.min(rem_ref[b]) < SKIP_LOG

    live = [unfinished(b) for b in range(Q_BLOCKS_PER_STEP)]
    for b in range(Q_BLOCKS_PER_STEP):
        def cond(state):
            j, go = state
            return jnp.logical_and(j >= 0, go)

        def body(state, b=b):
            j, _ = state
            visit([(b, [(j, False)])], first=False)
            return j - 1, unfinished(b)

        lax.while_loop(cond, body, (jnp.where(fused, first_block + b - 3, -1), live[b]))
    for b in range(Q_BLOCKS_PER_STEP):
        for p in pairs:
            o_ref[0, q_rows[b], lanes[p]] = acc_ref[b, p].astype(o_ref.dtype)


def _tri_matrix():
    bq = ATTN_BLOCK
    j = lax.broadcasted_iota(jnp.int32, (bq, bq), 0)
    s = lax.broadcasted_iota(jnp.int32, (bq, bq), 1)
    return jnp.concatenate([(j >= s).astype(_bf16), jnp.ones((bq, bq), _bf16)], axis=1)


def _attention(q, k, v):
    batch, seq, width = q.shape
    n_pairs = width // LANES
    bq = ATTN_BLOCK
    rows = Q_BLOCKS_PER_STEP * bq
    blk = pl.BlockSpec((1, rows, width), lambda b, i: (b, i, 0))
    whole = pl.BlockSpec((1, seq, width), lambda b, i: (b, 0, 0))
    return pl.pallas_call(
        _attn_kernel,
        grid=(batch, seq // rows),
        in_specs=[blk, whole, whole, _resident((bq, 2 * bq))],
        out_specs=blk,
        out_shape=jax.ShapeDtypeStruct((batch, seq, width), _bf16),
        scratch_shapes=[pltpu.VMEM((Q_BLOCKS_PER_STEP, n_pairs, bq, LANES), _f32),
                        pltpu.VMEM((Q_BLOCKS_PER_STEP, n_pairs, 2 * bq, bq), _f32)],
        compiler_params=pltpu.CompilerParams(
            dimension_semantics=("arbitrary", "arbitrary"), vmem_limit_bytes=VMEM_LIMIT),
        name="attn",
    )(q, k, v, _tri_matrix())


def _tail_kernel(o_ref, p1_ref, sb_ref, h_ref, mod_ref, wao_ref, wo_ref, g3_ref, wgu_ref, wd_ref,
                 gf_ref, out_ref, acc_ref):
    yb = _dot(o_ref[...], wao_ref[...])
    merged = p1_ref[...].astype(_f32) + sb_ref[...].astype(_f32) * yb
    y = _dot(merged.astype(_bf16), wo_ref[...])
    h2 = h_ref[...] + _gate(mod_ref, 1) * y
    u = _norm_mod(h2, _gain(g3_ref, mod_ref, 2), _shift(mod_ref, 2)).astype(_bf16)
    _swiglu_into(acc_ref, u, wgu_ref, wd_ref)
    h3 = h2 + (FFN_RES_WEIGHT * _gate(mod_ref, 2)) * acc_ref[...]
    inv = lax.rsqrt(jnp.mean(h3 * h3, axis=-1, keepdims=True) + EPS)
    out_ref[...] = (h3 * inv) * gf_ref[...]


def _tail(o, p1, sb, h, mod, w_attn_out, w_out, norm_g, wgu, wd, final_g, seq):
    t, d = h.shape
    tiles_per_seq = seq // ROW_TILE
    row = lambda width: pl.BlockSpec((ROW_TILE, width), lambda i: (i, 0))
    return pl.pallas_call(
        _tail_kernel,
        grid=(t // ROW_TILE,),
        in_specs=[
            row(ATTN_WIDTH), row(d), row(d), row(d),
            pl.BlockSpec((1, N_SUBLAYERS * N_MOD, d), lambda i: (i // tiles_per_seq, 0, 0)),
            _resident(w_attn_out.shape),
            _resident(w_out.shape),
            _resident((1, d)),
            _resident(wgu.shape),
            _resident(wd.shape),
            _resident((1, d)),
        ],
        out_specs=row(d),
        out_shape=jax.ShapeDtypeStruct((t, d), _f32),
        scratch_shapes=[pltpu.VMEM((ROW_TILE, d), _f32)],
        compiler_params=pltpu.CompilerParams(
            dimension_semantics=("arbitrary",), vmem_limit_bytes=VMEM_LIMIT),
        name="tail",
    )(o, p1, sb, h, mod, w_attn_out, w_out, norm_g, wgu, wd, final_g)


def kernel(x, c, w_ada, b_ada, norm1_g, ffn1_w_gu, ffn1_w_down, norm2_g, w_mix_in, b_merge, conv_w,
           w_conv_out, w_attn_out, w_out, norm3_g, ffn2_w_gu, ffn2_w_down, final_g):
    batch, seq, d = x.shape
    assert w_ada.shape[0] == 1
    assert seq % ROW_TILE == 0 and seq % (Q_BLOCKS_PER_STEP * ATTN_BLOCK) == 0 and d == D_MODEL
    h = x.reshape(batch * seq, d)
    mod = _adaln(c, w_ada[0], b_ada[0]).reshape(batch, N_SUBLAYERS * N_MOD, d)
    h1, (w_mix, w_co, w_ao, w_o, w_gu2, w_d2) = _ffn1(
        h, mod, norm1_g[0].reshape(1, d), ffn1_w_gu[0].astype(_bf16), ffn1_w_down[0].astype(_bf16),
        [w_mix_in[0], w_conv_out[0], w_attn_out[0], w_out[0], ffn2_w_gu[0], ffn2_w_down[0]], seq)
    q, k, v, p1, sb = _mixproj(h1, mod, norm2_g[0].reshape(1, d), w_mix, b_merge[0], conv_w[0],
                               w_co, seq)
    o = _attention(q.reshape(batch, seq, ATTN_WIDTH), k.reshape(batch, seq, ATTN_WIDTH),
                   v.reshape(batch, seq, ATTN_WIDTH)).reshape(batch * seq, ATTN_WIDTH)
    out = _tail(o, p1, sb, h1, mod, w_ao, w_o, norm3_g[0].reshape(1, d), w_gu2, w_d2,
                final_g.reshape(1, d), seq)
    return out.reshape(batch, seq, d)
```

```python
import functools

import jax
import jax.numpy as jnp
from jax import lax
from jax.experimental import pallas as pl
from jax.experimental.pallas import tpu as pltpu

D_MODEL = 1024
N_HEADS = 8
HEAD_DIM = 64
ATTN_WIDTH = N_HEADS * HEAD_DIM
CONV_WIDTH = 512
CONV_KSIZE = 3
D_FF = 2816
N_SUBLAYERS = 3
N_MOD = 3
EPS = 1e-6
FFN_RES_WEIGHT = 0.5
MIX_IN_WIDTH = 3 * CONV_WIDTH + 3 * ATTN_WIDTH + 2 * D_MODEL

LANES = 128
SUBLANES = 8
BF16_SUBLANES = 16
MXU_DIM = 256
VMEM_LIMIT = 56 * 1024 * 1024

ADALN_COLS = 1152
ROW_TILE = 512
PIECE_ROWS = 64
FF_CHUNK = MXU_DIM
N_FF_CHUNKS = D_FF // FF_CHUNK
ATTN_BLOCK = 128
Q_BLOCKS_PER_STEP = 4
SKIP_LOG = 106.0

assert D_FF % FF_CHUNK == 0

_bf16 = jnp.bfloat16
_f32 = jnp.float32


def _runtime_zero():
    return jnp.zeros((1,), jnp.int32)


def _dot(a, b):
    return jnp.dot(a, b, preferred_element_type=_f32)


def _split_bf16(x):
    hi = x.astype(_bf16)
    return hi, (x - hi.astype(_f32)).astype(_bf16)


def _resident(shape):
    zeros = (0,) * len(shape)
    return pl.BlockSpec(shape, lambda *_: zeros, pipeline_mode=pl.Buffered(1))


def _norm_mod(x, gain, shift):
    inv = lax.rsqrt(jnp.mean(x * x, axis=-1, keepdims=True) + EPS)
    return (x * inv) * gain + shift


def _gain(g_ref, mod_ref, sublayer):
    return g_ref[...] * (1.0 + mod_ref[0, N_MOD * sublayer + 1:N_MOD * sublayer + 2, :])


def _shift(mod_ref, sublayer):
    return mod_ref[0, N_MOD * sublayer:N_MOD * sublayer + 1, :]


def _gate(mod_ref, sublayer):
    return mod_ref[0, N_MOD * sublayer + 2:N_MOD * sublayer + 3, :]


def _token(value):
    return value[0:1, 0:LANES].astype(_f32)


def _tie(never, token, value):
    head = jnp.where(never, token, value[:, :LANES])
    if value.shape[1] == LANES:
        return head
    return jnp.concatenate([head, value[:, LANES:]], axis=1)


def _run_jobs(token, jobs):
    for job in jobs:
        token = job(token)
    return token


def _next_norm_jobs(never, hn_ref, un_ref, gain, shift):
    def job(k):
        rows = slice(k * PIECE_ROWS, (k + 1) * PIECE_ROWS)

        def run(token):
            piece = _norm_mod(hn_ref[rows, :], _tie(never, token, gain), shift).astype(_bf16)
            un_ref[rows, :] = piece
            return _token(piece)
        return run
    return [job(k) for k in range(ROW_TILE // PIECE_ROWS)]


def _swiglu_into(acc_ref, u_bf16, wgu_ref, wd_ref, never=None, side_jobs=()):
    pending = {}
    for c in range(N_FF_CHUNKS):
        cols = slice(c * FF_CHUNK, (c + 1) * FF_CHUNK)
        up_cols = slice(D_FF + c * FF_CHUNK, D_FF + (c + 1) * FF_CHUNK)
        g = _dot(u_bf16, wgu_ref[:, cols])
        if c - 2 in pending:
            g = _tie(never, pending.pop(c - 2), g)
        if c < len(side_jobs) and side_jobs[c]:
            pending[c] = _run_jobs(_token(g), side_jobs[c])
        up = _dot(u_bf16, wgu_ref[:, up_cols])
        act = (g * jax.nn.sigmoid(g) * up).astype(_bf16)
        down = _dot(act, wd_ref[cols, :])
        if c == 0:
            acc_ref[...] = down
        else:
            acc_ref[...] += down
    return list(pending.values())


def _adaln_kernel(n_cast, c_ref, w_ref, b_ref, *rest):
    cast_in = rest[:n_cast]
    o_ref = rest[n_cast]
    cast_out = rest[n_cast + 1:]
    c = c_ref[...]
    c_act = c * jax.nn.sigmoid(c)
    c_hi, c_lo = _split_bf16(c_act)
    w_hi, w_lo = _split_bf16(w_ref[...])
    o_ref[...] = _dot(c_hi, w_hi) + (_dot(c_lo, w_hi) + _dot(c_hi, w_lo)) + b_ref[...]
    for src, dst in zip(cast_in, cast_out):
        dst[...] = src[...].astype(_bf16)


def _cast_slab_spec(shape, n_steps):
    rows, cols = shape
    span = 1
    while (rows * span) % (n_steps * BF16_SUBLANES) != 0:
        span *= 2
    return pl.BlockSpec((rows * span // n_steps, cols), lambda i: (i // span, 0))


def _adaln(c, w_ada, b_ada, cast_weights):
    batch, d = c.shape
    n = w_ada.shape[1]
    n_steps = n // ADALN_COLS
    cast_specs = [_cast_slab_spec(w.shape, n_steps) for w in cast_weights]
    outs = pl.pallas_call(
        functools.partial(_adaln_kernel, len(cast_weights)),
        grid=(n_steps,),
        in_specs=[
            pl.BlockSpec((batch, d), lambda j: (0, 0)),
            pl.BlockSpec((d, ADALN_COLS), lambda j: (0, j)),
            pl.BlockSpec((1, ADALN_COLS), lambda j: (0, j)),
        ] + cast_specs,
        out_specs=[pl.BlockSpec((batch, ADALN_COLS), lambda j: (0, j))] + cast_specs,
        out_shape=[jax.ShapeDtypeStruct((batch, n), _f32)]
        + [jax.ShapeDtypeStruct(w.shape, _bf16) for w in cast_weights],
        compiler_params=pltpu.CompilerParams(
            dimension_semantics=("arbitrary",), vmem_limit_bytes=VMEM_LIMIT),
        name="adaln",
    )(c, w_ada, b_ada.reshape(1, n), *cast_weights)
    return outs[0], outs[1:]


def _ffn1_kernel(n_cast, zero_ref, h_ref, hn_ref, mod_ref, modn_ref, g_ref, wgu_ref, wd_ref, *rest):
    cast_in = rest[:n_cast]
    o_ref = rest[n_cast]
    cast_out = rest[n_cast + 1:2 * n_cast + 1]
    acc_ref, u_even_ref, u_odd_ref = rest[2 * n_cast + 1:]
    i = pl.program_id(0)
    never = zero_ref[0] != 0

    @pl.when(i == 0)
    def _():
        u_even_ref[...] = _norm_mod(h_ref[...], _gain(g_ref, mod_ref, 0), _shift(mod_ref, 0)).astype(_bf16)

    def cast_job(src, dst):
        def run(token):
            w = _tie(never, token, src[...]).astype(_bf16)
            dst[...] = w
            return _token(w)
        return run

    def step(u_ref, un_ref):
        norm_jobs = _next_norm_jobs(never, hn_ref, un_ref, _gain(g_ref, modn_ref, 0), _shift(modn_ref, 0))
        cast_jobs = [cast_job(src, dst) for src, dst in zip(cast_in, cast_out)]
        n_tail = N_FF_CHUNKS - len(norm_jobs)
        side_jobs = [[job] for job in norm_jobs] + [cast_jobs[r::n_tail] for r in range(n_tail)]
        tokens = _swiglu_into(acc_ref, u_ref[...], wgu_ref, wd_ref, never, side_jobs)
        acc = acc_ref[...]
        for token in tokens:
            acc = _tie(never, token, acc)
        o_ref[...] = h_ref[...] + (FFN_RES_WEIGHT * _gate(mod_ref, 0)) * acc

    @pl.when(i % 2 == 0)
    def _():
        step(u_even_ref, u_odd_ref)

    @pl.when(i % 2 == 1)
    def _():
        step(u_odd_ref, u_even_ref)


def _ffn1(h, mod, norm_g, wgu, wd, later_weights, seq):
    t, d = h.shape
    step = ROW_TILE
    tiles_per_seq = seq // step
    n_steps = t // step
    cast_specs = [_cast_slab_spec(w.shape, n_steps) for w in later_weights]
    row_spec = pl.BlockSpec((step, d), lambda i: (i, 0))
    nxt = lambda i: jnp.minimum(i + 1, n_steps - 1)
    mod_block = (1, N_SUBLAYERS * N_MOD, d)
    outs = pl.pallas_call(
        functools.partial(_ffn1_kernel, len(later_weights)),
        grid=(n_steps,),
        in_specs=[
            pl.BlockSpec(memory_space=pltpu.SMEM),
            row_spec,
            pl.BlockSpec((step, d), lambda i: (nxt(i), 0)),
            pl.BlockSpec(mod_block, lambda i: (i // tiles_per_seq, 0, 0)),
            pl.BlockSpec(mod_block, lambda i: (nxt(i) // tiles_per_seq, 0, 0)),
            _resident((1, d)),
            _resident(wgu.shape),
            _resident(wd.shape),
        ] + cast_specs,
        out_specs=[row_spec] + cast_specs,
        out_shape=[jax.ShapeDtypeStruct((t, d), _f32)]
        + [jax.ShapeDtypeStruct(w.shape, _bf16) for w in later_weights],
        scratch_shapes=[pltpu.VMEM((step, d), _f32), pltpu.VMEM((step, d), _bf16),
                        pltpu.VMEM((step, d), _bf16)],
        compiler_params=pltpu.CompilerParams(
            dimension_semantics=("arbitrary",), vmem_limit_bytes=VMEM_LIMIT),
        name="ffn1",
    )(_runtime_zero(), h, h, mod, mod, norm_g, wgu, wd, *later_weights)
    return outs[0], outs[1:]


def _mixproj_kernel(tiles_per_seq, zero_ref, h0_ref, hn_ref, mod_ref, modn_ref, g_ref, w_ref, bm_ref,
                    cw_ref, wco_ref, q_ref, k_ref, v_ref, p1_ref, sb_ref, vbuf_ref, u_even_ref, u_odd_ref):
    i = pl.program_id(0)
    rows = ROW_TILE
    never = zero_ref[0] != 0

    @pl.when(i == 0)
    def _():
        u_even_ref[...] = _norm_mod(h0_ref[...], _gain(g_ref, mod_ref, 1), _shift(mod_ref, 1)).astype(_bf16)

    @pl.when(i % tiles_per_seq == 0)
    def _():
        vbuf_ref[0:SUBLANES, :] = jnp.zeros((SUBLANES, CONV_WIDTH), _f32)

    def step(u_ref, un_ref):
        jobs = _next_norm_jobs(never, hn_ref, un_ref, _gain(g_ref, modn_ref, 1), _shift(modn_ref, 1))
        per_dot = len(jobs) // 4
        u = u_ref[...]
        qkv0 = 3 * CONV_WIDTH
        g0 = qkv0 + 3 * ATTN_WIDTH
        def proj(col, width):
            return _dot(u, w_ref[:, col:col + width])

        cc = proj(CONV_WIDTH, CONV_WIDTH)
        token = _run_jobs(_token(cc), jobs[0:per_dot])
        cv = cc * _tie(never, token, proj(2 * CONV_WIDTH, CONV_WIDTH))
        vbuf_ref[SUBLANES:SUBLANES + rows, :] = cv
        y = (cw_ref[0:1, :] * vbuf_ref[SUBLANES - 2:SUBLANES - 2 + rows, :]
             + cw_ref[1:2, :] * vbuf_ref[SUBLANES - 1:SUBLANES - 1 + rows, :]
             + cw_ref[2:3, :] * cv)
        vbuf_ref[0:SUBLANES, :] = cv[rows - SUBLANES:, :]
        token = _run_jobs(_token(cv), jobs[per_dot:2 * per_dot])
        cb = _tie(never, token, proj(0, CONV_WIDTH))
        conv = (cb * y).astype(_bf16)

        q_ref[...] = (proj(qkv0, ATTN_WIDTH) * (HEAD_DIM ** -0.5)).astype(_bf16)
        k = proj(qkv0 + ATTN_WIDTH, ATTN_WIDTH)
        token = _run_jobs(_token(k), jobs[2 * per_dot:3 * per_dot])
        k_ref[...] = k.astype(_bf16)
        v_ref[...] = _tie(never, token, proj(qkv0 + 2 * ATTN_WIDTH, ATTN_WIDTH)).astype(_bf16)

        pga = proj(g0, D_MODEL)
        token = _run_jobs(_token(pga), jobs[3 * per_dot:])
        sa = jax.nn.sigmoid(pga + bm_ref[0:1, :])
        pgb = _tie(never, token, proj(g0 + D_MODEL, D_MODEL))
        sb_ref[...] = jax.nn.sigmoid(pgb + bm_ref[1:2, :]).astype(_bf16)
        p1_ref[...] = (sa * _dot(conv, wco_ref[...])).astype(_bf16)

    @pl.when(i % 2 == 0)
    def _():
        step(u_even_ref, u_odd_ref)

    @pl.when(i % 2 == 1)
    def _():
        step(u_odd_ref, u_even_ref)


def _mixproj(h, mod, norm_g, w_mix, b_merge, conv_w, w_conv_out, seq):
    t, d = h.shape
    tiles_per_seq = seq // ROW_TILE
    n_steps = t // ROW_TILE
    nxt = lambda i: jnp.minimum(i + 1, n_steps - 1)
    row = lambda width: pl.BlockSpec((ROW_TILE, width), lambda i: (i, 0))
    mod_block = (1, N_SUBLAYERS * N_MOD, d)
    return pl.pallas_call(
        functools.partial(_mixproj_kernel, tiles_per_seq),
        grid=(n_steps,),
        in_specs=[
            pl.BlockSpec(memory_space=pltpu.SMEM),
            pl.BlockSpec((ROW_TILE, d), lambda i: (0, 0)),
            pl.BlockSpec((ROW_TILE, d), lambda i: (nxt(i), 0)),
            pl.BlockSpec(mod_block, lambda i: (i // tiles_per_seq, 0, 0)),
            pl.BlockSpec(mod_block, lambda i: (nxt(i) // tiles_per_seq, 0, 0)),
            _resident((1, d)),
            _resident(w_mix.shape),
            _resident(b_merge.shape),
            _resident(conv_w.shape),
            _resident(w_conv_out.shape),
        ],
        out_specs=[row(ATTN_WIDTH), row(ATTN_WIDTH), row(ATTN_WIDTH), row(d), row(d)],
        out_shape=[jax.ShapeDtypeStruct((t, ATTN_WIDTH), _bf16)] * 3
        + [jax.ShapeDtypeStruct((t, d), _bf16)] * 2,
        scratch_shapes=[pltpu.VMEM((SUBLANES + ROW_TILE, CONV_WIDTH), _f32),
                        pltpu.VMEM((ROW_TILE, d), _bf16), pltpu.VMEM((ROW_TILE, d), _bf16)],
        compiler_params=pltpu.CompilerParams(
            dimension_semantics=("arbitrary",), vmem_limit_bytes=VMEM_LIMIT),
        name="mixproj",
    )(_runtime_zero(), h, h, mod, mod, norm_g, w_mix, b_merge, conv_w, w_conv_out)


def _attn_kernel(q_ref, k_ref, v_ref, tri_ref, o_ref, acc_ref, rem_ref):
    step = pl.program_id(1)
    bq = ATTN_BLOCK
    n_pairs = q_ref.shape[2] // LANES
    lane = lax.broadcasted_iota(jnp.int32, (bq, LANES), 1).astype(_f32).astype(_bf16)
    first_head = lane < HEAD_DIM

    def split_heads(x):
        zero = jnp.zeros_like(x)
        return jnp.concatenate([jnp.where(first_head, x, zero), jnp.where(first_head, zero, x)], axis=0)

    row = lax.broadcasted_iota(jnp.int32, (2 * bq, bq), 0)
    col = lax.broadcasted_iota(jnp.int32, (2 * bq, bq), 1)
    strictly_causal = col < jnp.where(row >= bq, row - bq, row)

    pairs = range(n_pairs)
    lanes = [slice(p * LANES, (p + 1) * LANES) for p in pairs]
    q_rows = [slice(b * bq, (b + 1) * bq) for b in range(Q_BLOCKS_PER_STEP)]
    q2 = [[split_heads(q_ref[0, q_rows[b], lanes[p]]) for p in pairs] for b in range(Q_BLOCKS_PER_STEP)]

    def scores(b, p, start, n_blocks):
        return lax.dot_general(q2[b][p], k_ref[0, pl.ds(start, n_blocks * bq), lanes[p]],
                               (((1,), (1,)), ((), ())), preferred_element_type=_f32)

    def visit(plan, first):
        chains = []
        for b, blocks in plan:
            starts = [pl.multiple_of(j * bq, bq) for j, _ in blocks]
            if len(blocks) == 3:
                both = [scores(b, p, starts[2], 2) for p in pairs]
                zs = [[scores(b, p, starts[0], 1) for p in pairs],
                      [z[:, bq:] for z in both], [z[:, :bq] for z in both]]
            else:
                zs = [[scores(b, p, start, 1) for p in pairs] for start in starts]
            chains += [(b, start, diagonal, z) for start, (_, diagonal), z in zip(starts, blocks, zs)]
        sps = []
        for _, _, diagonal, z_block in chains:
            row_sp = []
            for z in z_block:
                sp = jnp.maximum(z, 0.0) + jnp.log(1.0 + jnp.exp(-jnp.abs(z)))
                if diagonal:
                    sp = jnp.where(strictly_causal, sp, 0.0)
                row_sp.append(sp.astype(_bf16))
            sps.append(row_sp)
        css = [[_dot(x, tri_ref[...]) for x in row_sp] for row_sp in sps]
        seen = {b: [None if first else rem_ref[b, p] for p in pairs] for b, _ in plan}
        weights = []
        for n, (b, _, diagonal, z_block) in enumerate(chains):
            row_w = []
            for p in pairs:
                log_a = z_block[p] - css[n][p][:, :bq]
                a = jnp.exp(log_a if seen[b][p] is None else log_a - seen[b][p])
                if diagonal:
                    a = jnp.where(strictly_causal, a, 0.0)
                a = a.astype(_bf16)
                row_w.append(jnp.concatenate([a[:bq], a[bq:]], axis=1))
                total = css[n][p][:, bq:]
                seen[b][p] = total if seen[b][p] is None else seen[b][p] + total
            weights.append(row_w)
        for b, _ in plan:
            for p in pairs:
                out = None
                for n, (cb, start, _, _) in enumerate(chains):
                    if cb != b:
                        continue
                    term = _dot(weights[n][p], split_heads(v_ref[0, pl.ds(start, bq), lanes[p]]))
                    out = term if out is None else out + term
                if first:
                    acc_ref[b, p] = out
                else:
                    acc_ref[b, p] += out
                rem_ref[b, p] = seen[b][p]

    first_block = step * Q_BLOCKS_PER_STEP
    fused = step >= 1

    @pl.when(fused)
    def _():
        visit([(b, [(first_block + b, True), (first_block + b - 1, False), (first_block + b - 2, False)])
               for b in reversed(range(Q_BLOCKS_PER_STEP))], first=True)

    @pl.when(jnp.logical_not(fused))
    def _():
        visit([(b, [(b, True)] + [(j, False) for j in reversed(range(b))])
               for b in reversed(range(Q_BLOCKS_PER_STEP))], first=True)

    def unfinished(b):
        return jnp.min(rem_ref[b]) < SKIP_LOG

    live = [unfinished(b) for b in range(Q_BLOCKS_PER_STEP)]
    for b in range(Q_BLOCKS_PER_STEP):
        def cond(state):
            j, go = state
            return jnp.logical_and(j >= 0, go)

        def body(state, b=b):
            j, _ = state
            visit([(b, [(j, False)])], first=False)
            return j - 1, unfinished(b)

        lax.while_loop(cond, body, (jnp.where(fused, first_block + b - 3, -1), live[b]))
    for b in range(Q_BLOCKS_PER_STEP):
        for p in pairs:
            o_ref[0, q_rows[b], lanes[p]] = acc_ref[b, p].astype(o_ref.dtype)


def _tri_matrix():
    bq = ATTN_BLOCK
    j = lax.broadcasted_iota(jnp.int32, (bq, bq), 0)
    s = lax.broadcasted_iota(jnp.int32, (bq, bq), 1)
    return jnp.concatenate([(j >= s).astype(_bf16), jnp.ones((bq, bq), _bf16)], axis=1)


def _attention(q, k, v):
    batch, seq, width = q.shape
    n_pairs = width // LANES
    bq = ATTN_BLOCK
    rows = Q_BLOCKS_PER_STEP * bq
    blk = pl.BlockSpec((1, rows, width), lambda b, i: (b, i, 0))
    whole = pl.BlockSpec((1, seq, width), lambda b, i: (b, 0, 0))
    return pl.pallas_call(
        _attn_kernel,
        grid=(batch, seq // rows),
        in_specs=[blk, whole, whole, _resident((bq, 2 * bq))],
        out_specs=blk,
        out_shape=jax.ShapeDtypeStruct((batch, seq, width), _bf16),
        scratch_shapes=[pltpu.VMEM((Q_BLOCKS_PER_STEP, n_pairs, bq, LANES), _f32),
                        pltpu.VMEM((Q_BLOCKS_PER_STEP, n_pairs, 2 * bq, bq), _f32)],
        compiler_params=pltpu.CompilerParams(
            dimension_semantics=("arbitrary", "arbitrary"), vmem_limit_bytes=VMEM_LIMIT),
        name="attn",
    )(q, k, v, _tri_matrix())


def _tail_kernel(o_ref, p1_ref, sb_ref, h_ref, mod_ref, wao_ref, wo_ref, g3_ref, wgu_ref, wd_ref,
                 gf_ref, out_ref, acc_ref):
    yb = _dot(o_ref[...], wao_ref[...])
    merged = p1_ref[...].astype(_f32) + sb_ref[...].astype(_f32) * yb
    y = _dot(merged.astype(_bf16), wo_ref[...])
    h2 = h_ref[...] + _gate(mod_ref, 1) * y
    u = _norm_mod(h2, _gain(g3_ref, mod_ref, 2), _shift(mod_ref, 2)).astype(_bf16)
    _swiglu_into(acc_ref, u, wgu_ref, wd_ref)
    h3 = h2 + (FFN_RES_WEIGHT * _gate(mod_ref, 2)) * acc_ref[...]
    inv = lax.rsqrt(jnp.mean(h3 * h3, axis=-1, keepdims=True) + EPS)
    out_ref[...] = (h3 * inv) * gf_ref[...]


def _tail(o, p1, sb, h, mod, w_attn_out, w_out, norm_g, wgu, wd, final_g, seq):
    t, d = h.shape
    tiles_per_seq = seq // ROW_TILE
    row = lambda width: pl.BlockSpec((ROW_TILE, width), lambda i: (i, 0))
    return pl.pallas_call(
        _tail_kernel,
        grid=(t // ROW_TILE,),
        in_specs=[
            row(ATTN_WIDTH), row(d), row(d), row(d),
            pl.BlockSpec((1, N_SUBLAYERS * N_MOD, d), lambda i: (i // tiles_per_seq, 0, 0)),
            _resident(w_attn_out.shape),
            _resident(w_out.shape),
            _resident((1, d)),
            _resident(wgu.shape),
            _resident(wd.shape),
            _resident((1, d)),
        ],
        out_specs=row(d),
        out_shape=jax.ShapeDtypeStruct((t, d), _f32),
        scratch_shapes=[pltpu.VMEM((ROW_TILE, d), _f32)],
        compiler_params=pltpu.CompilerParams(
            dimension_semantics=("arbitrary",), vmem_limit_bytes=VMEM_LIMIT),
        name="tail",
    )(o, p1, sb, h, mod, w_attn_out, w_out, norm_g, wgu, wd, final_g)


def kernel(x, c, w_ada, b_ada, norm1_g, ffn1_w_gu, ffn1_w_down, norm2_g, w_mix_in, b_merge, conv_w,
           w_conv_out, w_attn_out, w_out, norm3_g, ffn2_w_gu, ffn2_w_down, final_g):
    batch, seq, d = x.shape
    assert w_ada.shape[0] == 1
    assert seq % ROW_TILE == 0 and seq % (Q_BLOCKS_PER_STEP * ATTN_BLOCK) == 0 and d == D_MODEL
    h = x.reshape(batch * seq, d)
    mod, (w_gu1, w_d1) = _adaln(c, w_ada[0], b_ada[0], [ffn1_w_gu[0], ffn1_w_down[0]])
    mod = mod.reshape(batch, N_SUBLAYERS * N_MOD, d)
    h1, (w_mix, w_co, w_ao, w_o, w_gu2, w_d2) = _ffn1(
        h, mod, norm1_g[0].reshape(1, d), w_gu1, w_d1,
        [w_mix_in[0], w_conv_out[0], w_attn_out[0], w_out[0], ffn2_w_gu[0], ffn2_w_down[0]], seq)
    q, k, v, p1, sb = _mixproj(h1, mod, norm2_g[0].reshape(1, d), w_mix, b_merge[0], conv_w[0],
                               w_co, seq)
    o = _attention(q.reshape(batch, seq, ATTN_WIDTH), k.reshape(batch, seq, ATTN_WIDTH),
                   v.reshape(batch, seq, ATTN_WIDTH)).reshape(batch * seq, ATTN_WIDTH)
    out = _tail(o, p1, sb, h1, mod, w_ao, w_o, norm3_g[0].reshape(1, d), w_gu2, w_d2,
                final_g.reshape(1, d), seq)
    return out.reshape(batch, seq, d)
```

```python
import functools

import jax
import jax.numpy as jnp
from jax import lax
from jax.experimental import pallas as pl
from jax.experimental.pallas import tpu as pltpu

D_MODEL = 1024
N_HEADS = 8
HEAD_DIM = 64
ATTN_WIDTH = N_HEADS * HEAD_DIM
CONV_WIDTH = 512
CONV_KSIZE = 3
D_FF = 2816
N_SUBLAYERS = 3
N_MOD = 3
EPS = 1e-6
FFN_RES_WEIGHT = 0.5
MIX_IN_WIDTH = 3 * CONV_WIDTH + 3 * ATTN_WIDTH + 2 * D_MODEL

LANES = 128
SUBLANES = 8
BF16_SUBLANES = 16
MXU_DIM = 256
VMEM_LIMIT = 56 * 1024 * 1024

ADALN_COLS = 1152
ROW_TILE = 512
PIECE_ROWS = 64
FF_CHUNK = MXU_DIM
N_FF_CHUNKS = D_FF // FF_CHUNK
ATTN_BLOCK = 128
Q_BLOCKS_PER_STEP = 4
SKIP_LOG = 106.0

assert D_FF % FF_CHUNK == 0

_bf16 = jnp.bfloat16
_f32 = jnp.float32


def _runtime_zero():
    return jnp.zeros((1,), jnp.int32)


def _dot(a, b):
    return jnp.dot(a, b, preferred_element_type=_f32)


def _split_bf16(x):
    hi = x.astype(_bf16)
    return hi, (x - hi.astype(_f32)).astype(_bf16)


def _resident(shape):
    zeros = (0,) * len(shape)
    return pl.BlockSpec(shape, lambda *_: zeros, pipeline_mode=pl.Buffered(1))


def _norm_mod(x, gain, shift):
    inv = lax.rsqrt(jnp.mean(x * x, axis=-1, keepdims=True) + EPS)
    return (x * inv) * gain + shift


def _gain(g_ref, mod_ref, sublayer):
    return g_ref[...] * (1.0 + mod_ref[0, N_MOD * sublayer + 1:N_MOD * sublayer + 2, :])


def _shift(mod_ref, sublayer):
    return mod_ref[0, N_MOD * sublayer:N_MOD * sublayer + 1, :]


def _gate(mod_ref, sublayer):
    return mod_ref[0, N_MOD * sublayer + 2:N_MOD * sublayer + 3, :]


def _token(value):
    return value[0:1, 0:LANES].astype(_f32)


def _tie(never, token, value):
    head = jnp.where(never, token, value[:, :LANES])
    if value.shape[1] == LANES:
        return head
    return jnp.concatenate([head, value[:, LANES:]], axis=1)


def _run_jobs(token, jobs):
    for job in jobs:
        token = job(token)
    return token


def _next_norm_jobs(never, hn_ref, un_ref, gain, shift):
    def job(k):
        rows = slice(k * PIECE_ROWS, (k + 1) * PIECE_ROWS)

        def run(token):
            piece = _norm_mod(hn_ref[rows, :], _tie(never, token, gain), shift).astype(_bf16)
            un_ref[rows, :] = piece
            return _token(piece)
        return run
    return [job(k) for k in range(ROW_TILE // PIECE_ROWS)]


def _swiglu_into(acc_ref, u_bf16, wgu_ref, wd_ref, never=None, side_jobs=()):
    pending = {}
    for c in range(N_FF_CHUNKS):
        cols = slice(c * FF_CHUNK, (c + 1) * FF_CHUNK)
        up_cols = slice(D_FF + c * FF_CHUNK, D_FF + (c + 1) * FF_CHUNK)
        g = _dot(u_bf16, wgu_ref[:, cols])
        if c - 2 in pending:
            g = _tie(never, pending.pop(c - 2), g)
        if c < len(side_jobs) and side_jobs[c]:
            pending[c] = _run_jobs(_token(g), side_jobs[c])
        up = _dot(u_bf16, wgu_ref[:, up_cols])
        act = (g * jax.nn.sigmoid(g) * up).astype(_bf16)
        down = _dot(act, wd_ref[cols, :])
        if c == 0:
            acc_ref[...] = down
        else:
            acc_ref[...] += down
    return list(pending.values())


def _adaln_kernel(n_cast, c_ref, w_ref, b_ref, *rest):
    cast_in = rest[:n_cast]
    o_ref = rest[n_cast]
    cast_out = rest[n_cast + 1:]
    c = c_ref[...]
    c_act = c * jax.nn.sigmoid(c)
    c_hi, c_lo = _split_bf16(c_act)
    w_hi, w_lo = _split_bf16(w_ref[...])
    o_ref[...] = _dot(c_hi, w_hi) + (_dot(c_lo, w_hi) + _dot(c_hi, w_lo)) + b_ref[...]
    for src, dst in zip(cast_in, cast_out):
        dst[...] = src[...].astype(_bf16)


def _cast_slab_spec(shape, n_steps):
    rows, cols = shape
    span = 1
    while (rows * span) % (n_steps * BF16_SUBLANES) != 0:
        span *= 2
    return pl.BlockSpec((rows * span // n_steps, cols), lambda i: (i // span, 0))


def _adaln(c, w_ada, b_ada, cast_weights):
    batch, d = c.shape
    n = w_ada.shape[1]
    n_steps = n // ADALN_COLS
    cast_specs = [_cast_slab_spec(w.shape, n_steps) for w in cast_weights]
    outs = pl.pallas_call(
        functools.partial(_adaln_kernel, len(cast_weights)),
        grid=(n_steps,),
        in_specs=[
            pl.BlockSpec((batch, d), lambda j: (0, 0)),
            pl.BlockSpec((d, ADALN_COLS), lambda j: (0, j)),
            pl.BlockSpec((1, ADALN_COLS), lambda j: (0, j)),
        ] + cast_specs,
        out_specs=[pl.BlockSpec((batch, ADALN_COLS), lambda j: (0, j))] + cast_specs,
        out_shape=[jax.ShapeDtypeStruct((batch, n), _f32)]
        + [jax.ShapeDtypeStruct(w.shape, _bf16) for w in cast_weights],
        compiler_params=pltpu.CompilerParams(
            dimension_semantics=("arbitrary",), vmem_limit_bytes=VMEM_LIMIT),
        name="adaln",
    )(c, w_ada, b_ada.reshape(1, n), *cast_weights)
    return outs[0], outs[1:]


def _ffn1_kernel(n_cast, zero_ref, h_ref, hn_ref, mod_ref, modn_ref, g_ref, wgu_ref, wd_ref, *rest):
    cast_in = rest[:n_cast]
    o_ref = rest[n_cast]
    cast_out = rest[n_cast + 1:2 * n_cast + 1]
    acc_ref, u_even_ref, u_odd_ref = rest[2 * n_cast + 1:]
    i = pl.program_id(0)
    never = zero_ref[0] != 0

    @pl.when(i == 0)
    def _():
        u_even_ref[...] = _norm_mod(h_ref[...], _gain(g_ref, mod_ref, 0), _shift(mod_ref, 0)).astype(_bf16)

    def cast_job(src, dst):
        def run(token):
            w = _tie(never, token, src[...]).astype(_bf16)
            dst[...] = w
            return _token(w)
        return run

    def step(u_ref, un_ref):
        norm_jobs = _next_norm_jobs(never, hn_ref, un_ref, _gain(g_ref, modn_ref, 0), _shift(modn_ref, 0))
        cast_jobs = [cast_job(src, dst) for src, dst in zip(cast_in, cast_out)]
        n_tail = N_FF_CHUNKS - len(norm_jobs)
        side_jobs = [[job] for job in norm_jobs] + [cast_jobs[r::n_tail] for r in range(n_tail)]
        tokens = _swiglu_into(acc_ref, u_ref[...], wgu_ref, wd_ref, never, side_jobs)
        acc = acc_ref[...]
        for token in tokens:
            acc = _tie(never, token, acc)
        o_ref[...] = h_ref[...] + (FFN_RES_WEIGHT * _gate(mod_ref, 0)) * acc

    @pl.when(i % 2 == 0)
    def _():
        step(u_even_ref, u_odd_ref)

    @pl.when(i % 2 == 1)
    def _():
        step(u_odd_ref, u_even_ref)


def _ffn1(h, mod, norm_g, wgu, wd, later_weights, seq):
    t, d = h.shape
    step = ROW_TILE
    tiles_per_seq = seq // step
    n_steps = t // step
    cast_specs = [_cast_slab_spec(w.shape, n_steps) for w in later_weights]
    row_spec = pl.BlockSpec((step, d), lambda i: (i, 0))
    nxt = lambda i: jnp.minimum(i + 1, n_steps - 1)
    mod_block = (1, N_SUBLAYERS * N_MOD, d)
    outs = pl.pallas_call(
        functools.partial(_ffn1_kernel, len(later_weights)),
        grid=(n_steps,),
        in_specs=[
            pl.BlockSpec(memory_space=pltpu.SMEM),
            row_spec,
            pl.BlockSpec((step, d), lambda i: (nxt(i), 0)),
            pl.BlockSpec(mod_block, lambda i: (i // tiles_per_seq, 0, 0)),
            pl.BlockSpec(mod_block, lambda i: (nxt(i) // tiles_per_seq, 0, 0)),
            _resident((1, d)),
            _resident(wgu.shape),
            _resident(wd.shape),
        ] + cast_specs,
        out_specs=[row_spec] + cast_specs,
        out_shape=[jax.ShapeDtypeStruct((t, d), _f32)]
        + [jax.ShapeDtypeStruct(w.shape, _bf16) for w in later_weights],
        scratch_shapes=[pltpu.VMEM((step, d), _f32), pltpu.VMEM((step, d), _bf16),
                        pltpu.VMEM((step, d), _bf16)],
        compiler_params=pltpu.CompilerParams(
            dimension_semantics=("arbitrary",), vmem_limit_bytes=VMEM_LIMIT),
        name="ffn1",
    )(_runtime_zero(), h, h, mod, mod, norm_g, wgu, wd, *later_weights)
    return outs[0], outs[1:]


def _mixproj_kernel(tiles_per_seq, zero_ref, h0_ref, hn_ref, mod_ref, modn_ref, g_ref, w_ref, bm_ref,
                    cw_ref, wco_ref, q_ref, k_ref, v_ref, p1_ref, sb_ref, vbuf_ref, u_even_ref, u_odd_ref):
    i = pl.program_id(0)
    rows = ROW_TILE
    never = zero_ref[0] != 0

    @pl.when(i == 0)
    def _():
        u_even_ref[...] = _norm_mod(h0_ref[...], _gain(g_ref, mod_ref, 1), _shift(mod_ref, 1)).astype(_bf16)

    @pl.when(i % tiles_per_seq == 0)
    def _():
        vbuf_ref[0:SUBLANES, :] = jnp.zeros((SUBLANES, CONV_WIDTH), _f32)

    def step(u_ref, un_ref):
        jobs = _next_norm_jobs(never, hn_ref, un_ref, _gain(g_ref, modn_ref, 1), _shift(modn_ref, 1))
        per_dot = len(jobs) // 4
        u = u_ref[...]
        qkv0 = 3 * CONV_WIDTH
        g0 = qkv0 + 3 * ATTN_WIDTH
        def proj(col, width):
            return _dot(u, w_ref[:, col:col + width])

        cc = proj(CONV_WIDTH, CONV_WIDTH)
        token = _run_jobs(_token(cc), jobs[0:per_dot])
        cv = cc * _tie(never, token, proj(2 * CONV_WIDTH, CONV_WIDTH))
        vbuf_ref[SUBLANES:SUBLANES + rows, :] = cv
        y = (cw_ref[0:1, :] * vbuf_ref[SUBLANES - 2:SUBLANES - 2 + rows, :]
             + cw_ref[1:2, :] * vbuf_ref[SUBLANES - 1:SUBLANES - 1 + rows, :]
             + cw_ref[2:3, :] * cv)
        vbuf_ref[0:SUBLANES, :] = cv[rows - SUBLANES:, :]
        token = _run_jobs(_token(cv), jobs[per_dot:2 * per_dot])
        cb = _tie(never, token, proj(0, CONV_WIDTH))
        conv = (cb * y).astype(_bf16)

        q_ref[...] = (proj(qkv0, ATTN_WIDTH) * (HEAD_DIM ** -0.5)).astype(_bf16)
        k = proj(qkv0 + ATTN_WIDTH, ATTN_WIDTH)
        token = _run_jobs(_token(k), jobs[2 * per_dot:3 * per_dot])
        k_ref[...] = k.astype(_bf16)
        v_ref[...] = _tie(never, token, proj(qkv0 + 2 * ATTN_WIDTH, ATTN_WIDTH)).astype(_bf16)

        pga = proj(g0, D_MODEL)
        token = _run_jobs(_token(pga), jobs[3 * per_dot:])
        sa = jax.nn.sigmoid(pga + bm_ref[0:1, :])
        pgb = _tie(never, token, proj(g0 + D_MODEL, D_MODEL))
        sb_ref[...] = jax.nn.sigmoid(pgb + bm_ref[1:2, :]).astype(_bf16)
        p1_ref[...] = (sa * _dot(conv, wco_ref[...])).astype(_bf16)

    @pl.when(i % 2 == 0)
    def _():
        step(u_even_ref, u_odd_ref)

    @pl.when(i % 2 == 1)
    def _():
        step(u_odd_ref, u_even_ref)


def _mixproj(h, mod, norm_g, w_mix, b_merge, conv_w, w_conv_out, seq):
    t, d = h.shape
    tiles_per_seq = seq // ROW_TILE
    n_steps = t // ROW_TILE
    nxt = lambda i: jnp.minimum(i + 1, n_steps - 1)
    row = lambda width: pl.BlockSpec((ROW_TILE, width), lambda i: (i, 0))
    mod_block = (1, N_SUBLAYERS * N_MOD, d)
    return pl.pallas_call(
        functools.partial(_mixproj_kernel, tiles_per_seq),
        grid=(n_steps,),
        in_specs=[
            pl.BlockSpec(memory_space=pltpu.SMEM),
            pl.BlockSpec((ROW_TILE, d), lambda i: (0, 0)),
            pl.BlockSpec((ROW_TILE, d), lambda i: (nxt(i), 0)),
            pl.BlockSpec(mod_block, lambda i: (i // tiles_per_seq, 0, 0)),
            pl.BlockSpec(mod_block, lambda i: (nxt(i) // tiles_per_seq, 0, 0)),
            _resident((1, d)),
            _resident(w_mix.shape),
            _resident(b_merge.shape),
            _resident(conv_w.shape),
            _resident(w_conv_out.shape),
        ],
        out_specs=[row(ATTN_WIDTH), row(ATTN_WIDTH), row(ATTN_WIDTH), row(d), row(d)],
        out_shape=[jax.ShapeDtypeStruct((t, ATTN_WIDTH), _bf16)] * 3
        + [jax.ShapeDtypeStruct((t, d), _bf16)] * 2,
        scratch_shapes=[pltpu.VMEM((SUBLANES + ROW_TILE, CONV_WIDTH), _f32),
                        pltpu.VMEM((ROW_TILE, d), _bf16), pltpu.VMEM((ROW_TILE, d), _bf16)],
        compiler_params=pltpu.CompilerParams(
            dimension_semantics=("arbitrary",), vmem_limit_bytes=VMEM_LIMIT),
        name="mixproj",
    )(_runtime_zero(), h, h, mod, mod, norm_g, w_mix, b_merge, conv_w, w_conv_out)


def _attn_kernel(q_ref, k_ref, v_ref, tri_ref, o_ref, acc_ref, rem_ref):
    step = pl.program_id(1)
    bq = ATTN_BLOCK
    n_pairs = q_ref.shape[2] // LANES
    lane = lax.broadcasted_iota(jnp.int32, (bq, LANES), 1).astype(_f32).astype(_bf16)
    first_head = lane < HEAD_DIM

    def split_heads(x):
        zero = jnp.zeros_like(x)
        return jnp.concatenate([jnp.where(first_head, x, zero), jnp.where(first_head, zero, x)], axis=0)

    row = lax.broadcasted_iota(jnp.int32, (2 * bq, bq), 0)
    col = lax.broadcasted_iota(jnp.int32, (2 * bq, bq), 1)
    strictly_causal = col < jnp.where(row >= bq, row - bq, row)

    pairs = range(n_pairs)
    lanes = [slice(p * LANES, (p + 1) * LANES) for p in pairs]
    q_rows = [slice(b * bq, (b + 1) * bq) for b in range(Q_BLOCKS_PER_STEP)]
    q2 = [[split_heads(q_ref[0, q_rows[b], lanes[p]]) for p in pairs] for b in range(Q_BLOCKS_PER_STEP)]

    def scores(b, p, start, n_blocks):
        return lax.dot_general(q2[b][p], k_ref[0, pl.ds(start, n_blocks * bq), lanes[p]],
                               (((1,), (1,)), ((), ())), preferred_element_type=_f32)

    def visit(plan, first):
        chains = []
        for b, blocks in plan:
            starts = [pl.multiple_of(j * bq, bq) for j, _ in blocks]
            if len(blocks) == 3:
                both = [scores(b, p, starts[2], 2) for p in pairs]
                zs = [[scores(b, p, starts[0], 1) for p in pairs],
                      [z[:, bq:] for z in both], [z[:, :bq] for z in both]]
            else:
                zs = [[scores(b, p, start, 1) for p in pairs] for start in starts]
            chains += [(b, start, diagonal, z) for start, (_, diagonal), z in zip(starts, blocks, zs)]
        sps = []
        for _, _, diagonal, z_block in chains:
            row_sp = []
            for z in z_block:
                sp = jnp.maximum(z, 0.0) + jnp.log(1.0 + jnp.exp(-jnp.abs(z)))
                if diagonal:
                    sp = jnp.where(strictly_causal, sp, 0.0)
                row_sp.append(sp.astype(_bf16))
            sps.append(row_sp)
        css = [[_dot(x, tri_ref[...]) for x in row_sp] for row_sp in sps]
        seen = {b: [None if first else rem_ref[b, p] for p in pairs] for b, _ in plan}
        weights = []
        for n, (b, _, diagonal, z_block) in enumerate(chains):
            row_w = []
            for p in pairs:
                log_a = z_block[p] - css[n][p][:, :bq]
                a = jnp.exp(log_a if seen[b][p] is None else log_a - seen[b][p])
                if diagonal:
                    a = jnp.where(strictly_causal, a, 0.0)
                a = a.astype(_bf16)
                row_w.append(jnp.concatenate([a[:bq], a[bq:]], axis=1))
                total = css[n][p][:, bq:]
                seen[b][p] = total if seen[b][p] is None else seen[b][p] + total
            weights.append(row_w)
        for b, _ in plan:
            for p in pairs:
                out = None
                for n, (cb, start, _, _) in enumerate(chains):
                    if cb != b:
                        continue
                    term = _dot(weights[n][p], split_heads(v_ref[0, pl.ds(start, bq), lanes[p]]))
                    out = term if out is None else out + term
                if first:
                    acc_ref[b, p] = out
                else:
                    acc_ref[b, p] += out
                rem_ref[b, p] = seen[b][p]

    first_block = step * Q_BLOCKS_PER_STEP
    fused = step >= 1

    @pl.when(fused)
    def _():
        visit([(b, [(first_block + b, True), (first_block + b - 1, False), (first_block + b - 2, False)])
               for b in reversed(range(Q_BLOCKS_PER_STEP))], first=True)

    @pl.when(jnp.logical_not(fused))
    def _():
        visit([(b, [(b, True)] + [(j, False) for j in reversed(range(b))])
               for b in reversed(range(Q_BLOCKS_PER_STEP))], first=True)

    def unfinished(b):
        return jnp.min(rem_ref[b]) < SKIP_LOG

    live = [unfinished(b) for b in range(Q_BLOCKS_PER_STEP)]
    for b in range(Q_BLOCKS_PER_STEP):
        def cond(state):
            j, go = state
            return jnp.logical_and(j >= 0, go)

        def body(state, b=b):
            j, _ = state
            visit([(b, [(j, False)])], first=False)
            return j - 1, unfinished(b)

        lax.while_loop(cond, body, (jnp.where(fused, first_block + b - 3, -1), live[b]))
    for b in range(Q_BLOCKS_PER_STEP):
        for p in pairs:
            o_ref[0, q_rows[b], lanes[p]] = acc_ref[b, p].astype(o_ref.dtype)


def _tri_matrix():
    bq = ATTN_BLOCK
    j = lax.broadcasted_iota(jnp.int32, (bq, bq), 0)
    s = lax.broadcasted_iota(jnp.int32, (bq, bq), 1)
    return jnp.concatenate([(j >= s).astype(_bf16), jnp.ones((bq, bq), _bf16)], axis=1)


def _attention(q, k, v):
    batch, seq, width = q.shape
    n_pairs = width // LANES
    bq = ATTN_BLOCK
    rows = Q_BLOCKS_PER_STEP * bq
    blk = pl.BlockSpec((1, rows, width), lambda b, i: (b, i, 0))
    whole = pl.BlockSpec((1, seq, width), lambda b, i: (b, 0, 0))
    return pl.pallas_call(
        _attn_kernel,
        grid=(batch, seq // rows),
        in_specs=[blk, whole, whole, _resident((bq, 2 * bq))],
        out_specs=blk,
        out_shape=jax.ShapeDtypeStruct((batch, seq, width), _bf16),
        scratch_shapes=[pltpu.VMEM((Q_BLOCKS_PER_STEP, n_pairs, bq, LANES), _f32),
                        pltpu.VMEM((Q_BLOCKS_PER_STEP, n_pairs, 2 * bq, bq), _f32)],
        compiler_params=pltpu.CompilerParams(
            dimension_semantics=("arbitrary", "arbitrary"), vmem_limit_bytes=VMEM_LIMIT),
        name="attn",
    )(q, k, v, _tri_matrix())


def _tail_kernel(o_ref, p1_ref, sb_ref, h_ref, mod_ref, wao_ref, wo_ref, g3_ref, wgu_ref, wd_ref,
                 gf_ref, out_ref, acc_ref):
    halves = [slice(r, r + ROW_TILE // 2) for r in (0, ROW_TILE // 2)]
    yb = [_dot(o_ref[rows, :], wao_ref[...]) for rows in halves]
    merged = [(p1_ref[rows, :].astype(_f32) + sb_ref[rows, :].astype(_f32) * yb_h).astype(_bf16)
              for rows, yb_h in zip(halves, yb)]
    y = [_dot(m, wo_ref[...]) for m in merged]
    h2 = jnp.concatenate([h_ref[rows, :] + _gate(mod_ref, 1) * y_h for rows, y_h in zip(halves, y)], axis=0)
    u = _norm_mod(h2, _gain(g3_ref, mod_ref, 2), _shift(mod_ref, 2)).astype(_bf16)
    _swiglu_into(acc_ref, u, wgu_ref, wd_ref)
    h3 = h2 + (FFN_RES_WEIGHT * _gate(mod_ref, 2)) * acc_ref[...]
    inv = lax.rsqrt(jnp.mean(h3 * h3, axis=-1, keepdims=True) + EPS)
    out_ref[...] = (h3 * inv) * gf_ref[...]


def _tail(o, p1, sb, h, mod, w_attn_out, w_out, norm_g, wgu, wd, final_g, seq):
    t, d = h.shape
    tiles_per_seq = seq // ROW_TILE
    row = lambda width: pl.BlockSpec((ROW_TILE, width), lambda i: (i, 0))
    return pl.pallas_call(
        _tail_kernel,
        grid=(t // ROW_TILE,),
        in_specs=[
            row(ATTN_WIDTH), row(d), row(d), row(d),
            pl.BlockSpec((1, N_SUBLAYERS * N_MOD, d), lambda i: (i // tiles_per_seq, 0, 0)),
            _resident(w_attn_out.shape),
            _resident(w_out.shape),
            _resident((1, d)),
            _resident(wgu.shape),
            _resident(wd.shape),
            _resident((1, d)),
        ],
        out_specs=row(d),
        out_shape=jax.ShapeDtypeStruct((t, d), _f32),
        scratch_shapes=[pltpu.VMEM((ROW_TILE, d), _f32)],
        compiler_params=pltpu.CompilerParams(
            dimension_semantics=("arbitrary",), vmem_limit_bytes=VMEM_LIMIT),
        name="tail",
    )(o, p1, sb, h, mod, w_attn_out, w_out, norm_g, wgu, wd, final_g)


def kernel(x, c, w_ada, b_ada, norm1_g, ffn1_w_gu, ffn1_w_down, norm2_g, w_mix_in, b_merge, conv_w,
           w_conv_out, w_attn_out, w_out, norm3_g, ffn2_w_gu, ffn2_w_down, final_g):
    batch, seq, d = x.shape
    assert w_ada.shape[0] == 1
    assert seq % ROW_TILE == 0 and seq % (Q_BLOCKS_PER_STEP * ATTN_BLOCK) == 0 and d == D_MODEL
    assert w_mix_in.shape[1:] == (d, MIX_IN_WIDTH) and conv_w.shape[1:] == (CONV_KSIZE, CONV_WIDTH)
    assert ROW_TILE // PIECE_ROWS <= N_FF_CHUNKS and (ROW_TILE // PIECE_ROWS) % 4 == 0
    h = x.reshape(batch * seq, d)
    mod, (w_gu1, w_d1) = _adaln(c, w_ada[0], b_ada[0], [ffn1_w_gu[0], ffn1_w_down[0]])
    mod = mod.reshape(batch, N_SUBLAYERS * N_MOD, d)
    h1, (w_mix, w_co, w_ao, w_o, w_gu2, w_d2) = _ffn1(
        h, mod, norm1_g[0].reshape(1, d), w_gu1, w_d1,
        [w_mix_in[0], w_conv_out[0], w_attn_out[0], w_out[0], ffn2_w_gu[0], ffn2_w_down[0]], seq)
    q, k, v, p1, sb = _mixproj(h1, mod, norm2_g[0].reshape(1, d), w_mix, b_merge[0], conv_w[0],
                               w_co, seq)
    o = _attention(q.reshape(batch, seq, ATTN_WIDTH), k.reshape(batch, seq, ATTN_WIDTH),
                   v.reshape(batch, seq, ATTN_WIDTH)).reshape(batch * seq, ATTN_WIDTH)
    out = _tail(o, p1, sb, h1, mod, w_ao, w_o, norm3_g[0].reshape(1, d), w_gu2, w_d2,
                final_g.reshape(1, d), seq)
    return out.reshape(batch, seq, d)
```

```python
import functools

import jax
import jax.numpy as jnp
from jax import lax
from jax.experimental import pallas as pl
from jax.experimental.pallas import tpu as pltpu

D_MODEL = 1024
N_HEADS = 8
HEAD_DIM = 64
ATTN_WIDTH = N_HEADS * HEAD_DIM
CONV_WIDTH = 512
CONV_KSIZE = 3
D_FF = 2816
N_SUBLAYERS = 3
N_MOD = 3
EPS = 1e-6
FFN_RES_WEIGHT = 0.5
MIX_IN_WIDTH = 3 * CONV_WIDTH + 3 * ATTN_WIDTH + 2 * D_MODEL

LANES = 128
SUBLANES = 8
BF16_SUBLANES = 16
MXU_DIM = 256
VMEM_LIMIT = 56 * 1024 * 1024

ADALN_COLS = 1152
ROW_TILE = 512
TAIL_ROWS = 1024
TAIL_VMEM_LIMIT = 60 * 1024 * 1024
PIECE_ROWS = 64
FF_CHUNK = MXU_DIM
N_FF_CHUNKS = D_FF // FF_CHUNK
DOWN_GROUP = 3
ATTN_BLOCK = 128
Q_BLOCKS_PER_STEP = 4
SKIP_LOG = 106.0

assert D_FF % FF_CHUNK == 0

_bf16 = jnp.bfloat16
_f32 = jnp.float32


def _runtime_zero():
    return jnp.zeros((1,), jnp.int32)


def _dot(a, b):
    return jnp.dot(a, b, preferred_element_type=_f32)


def _split_bf16(x):
    hi = x.astype(_bf16)
    return hi, (x - hi.astype(_f32)).astype(_bf16)


def _resident(shape):
    zeros = (0,) * len(shape)
    return pl.BlockSpec(shape, lambda *_: zeros, pipeline_mode=pl.Buffered(1))


def _norm_mod(x, gain, shift):
    inv = lax.rsqrt(jnp.mean(x * x, axis=-1, keepdims=True) + EPS)
    return (x * inv) * gain + shift


def _gain(g_ref, mod_ref, sublayer):
    return g_ref[...] * (1.0 + mod_ref[0, N_MOD * sublayer + 1:N_MOD * sublayer + 2, :])


def _shift(mod_ref, sublayer):
    return mod_ref[0, N_MOD * sublayer:N_MOD * sublayer + 1, :]


def _gate(mod_ref, sublayer):
    return mod_ref[0, N_MOD * sublayer + 2:N_MOD * sublayer + 3, :]


def _token(value):
    return value[0:1, 0:LANES].astype(_f32)


def _tie(never, token, value):
    head = jnp.where(never, token, value[:, :LANES])
    if value.shape[1] == LANES:
        return head
    return jnp.concatenate([head, value[:, LANES:]], axis=1)


def _run_jobs(token, jobs):
    for job in jobs:
        token = job(token)
    return token


def _next_norm_jobs(never, hn_ref, un_ref, gain, shift):
    def job(k):
        rows = slice(k * PIECE_ROWS, (k + 1) * PIECE_ROWS)

        def run(token):
            piece = _norm_mod(hn_ref[rows, :], _tie(never, token, gain), shift).astype(_bf16)
            un_ref[rows, :] = piece
            return _token(piece)
        return run
    return [job(k) for k in range(ROW_TILE // PIECE_ROWS)]


def _swiglu_into(acc_ref, u_bf16, wgu_ref, wd_ref, never=None, side_jobs=()):
    pending = {}
    acts = []
    for c in range(N_FF_CHUNKS):
        cols = slice(c * FF_CHUNK, (c + 1) * FF_CHUNK)
        up_cols = slice(D_FF + c * FF_CHUNK, D_FF + (c + 1) * FF_CHUNK)
        g = _dot(u_bf16, wgu_ref[:, cols])
        if c - 2 in pending:
            g = _tie(never, pending.pop(c - 2), g)
        if c < len(side_jobs) and side_jobs[c]:
            pending[c] = _run_jobs(_token(g), side_jobs[c])
        up = _dot(u_bf16, wgu_ref[:, up_cols])
        acts.append((g * jax.nn.sigmoid(g) * up).astype(_bf16))
        if len(acts) == DOWN_GROUP or c == N_FF_CHUNKS - 1:
            first_col = (c + 1 - len(acts)) * FF_CHUNK
            act = acts[0] if len(acts) == 1 else jnp.concatenate(acts, axis=1)
            down = _dot(act, wd_ref[first_col:(c + 1) * FF_CHUNK, :])
            if first_col == 0:
                acc_ref[...] = down
            else:
                acc_ref[...] += down
            acts = []
    return list(pending.values())


def _adaln_kernel(n_cast, c_ref, w_ref, b_ref, *rest):
    cast_in = rest[:n_cast]
    o_ref = rest[n_cast]
    cast_out = rest[n_cast + 1:]
    c = c_ref[...]
    c_act = c * jax.nn.sigmoid(c)
    c_hi, c_lo = _split_bf16(c_act)
    w_hi, w_lo = _split_bf16(w_ref[...])
    o_ref[...] = _dot(c_hi, w_hi) + (_dot(c_lo, w_hi) + _dot(c_hi, w_lo)) + b_ref[...]
    for src, dst in zip(cast_in, cast_out):
        dst[...] = src[...].astype(_bf16)


def _cast_slab_spec(shape, n_steps):
    rows, cols = shape
    span = 1
    while (rows * span) % (n_steps * BF16_SUBLANES) != 0:
        span *= 2
    return pl.BlockSpec((rows * span // n_steps, cols), lambda i: (i // span, 0))


def _adaln(c, w_ada, b_ada, cast_weights):
    batch, d = c.shape
    n = w_ada.shape[1]
    n_steps = n // ADALN_COLS
    cast_specs = [_cast_slab_spec(w.shape, n_steps) for w in cast_weights]
    outs = pl.pallas_call(
        functools.partial(_adaln_kernel, len(cast_weights)),
        grid=(n_steps,),
        in_specs=[
            pl.BlockSpec((batch, d), lambda j: (0, 0)),
            pl.BlockSpec((d, ADALN_COLS), lambda j: (0, j)),
            pl.BlockSpec((1, ADALN_COLS), lambda j: (0, j)),
        ] + cast_specs,
        out_specs=[pl.BlockSpec((batch, ADALN_COLS), lambda j: (0, j))] + cast_specs,
        out_shape=[jax.ShapeDtypeStruct((batch, n), _f32)]
        + [jax.ShapeDtypeStruct(w.shape, _bf16) for w in cast_weights],
        compiler_params=pltpu.CompilerParams(
            dimension_semantics=("arbitrary",), vmem_limit_bytes=VMEM_LIMIT),
        name="adaln",
    )(c, w_ada, b_ada.reshape(1, n), *cast_weights)
    return outs[0], outs[1:]


def _ffn1_kernel(n_cast, zero_ref, h_ref, hn_ref, mod_ref, modn_ref, g_ref, wgu_ref, wd_ref, *rest):
    cast_in = rest[:n_cast]
    o_ref = rest[n_cast]
    cast_out = rest[n_cast + 1:2 * n_cast + 1]
    acc_ref, u_even_ref, u_odd_ref = rest[2 * n_cast + 1:]
    i = pl.program_id(0)
    never = zero_ref[0] != 0

    @pl.when(i == 0)
    def _():
        u_even_ref[...] = _norm_mod(h_ref[...], _gain(g_ref, mod_ref, 0), _shift(mod_ref, 0)).astype(_bf16)

    def cast_job(src, dst):
        def run(token):
            w = _tie(never, token, src[...]).astype(_bf16)
            dst[...] = w
            return _token(w)
        return run

    def step(u_ref, un_ref):
        norm_jobs = _next_norm_jobs(never, hn_ref, un_ref, _gain(g_ref, modn_ref, 0), _shift(modn_ref, 0))
        cast_jobs = [cast_job(src, dst) for src, dst in zip(cast_in, cast_out)]
        n_tail = N_FF_CHUNKS - len(norm_jobs)
        side_jobs = [[job] for job in norm_jobs] + [cast_jobs[r::n_tail] for r in range(n_tail)]
        tokens = _swiglu_into(acc_ref, u_ref[...], wgu_ref, wd_ref, never, side_jobs)
        acc = acc_ref[...]
        for token in tokens:
            acc = _tie(never, token, acc)
        o_ref[...] = h_ref[...] + (FFN_RES_WEIGHT * _gate(mod_ref, 0)) * acc

    @pl.when(i % 2 == 0)
    def _():
        step(u_even_ref, u_odd_ref)

    @pl.when(i % 2 == 1)
    def _():
        step(u_odd_ref, u_even_ref)


def _ffn1(h, mod, norm_g, wgu, wd, later_weights, seq):
    t, d = h.shape
    step = ROW_TILE
    tiles_per_seq = seq // step
    n_steps = t // step
    cast_specs = [_cast_slab_spec(w.shape, n_steps) for w in later_weights]
    row_spec = pl.BlockSpec((step, d), lambda i: (i, 0))
    nxt = lambda i: jnp.minimum(i + 1, n_steps - 1)
    mod_block = (1, N_SUBLAYERS * N_MOD, d)
    outs = pl.pallas_call(
        functools.partial(_ffn1_kernel, len(later_weights)),
        grid=(n_steps,),
        in_specs=[
            pl.BlockSpec(memory_space=pltpu.SMEM),
            row_spec,
            pl.BlockSpec((step, d), lambda i: (nxt(i), 0)),
            pl.BlockSpec(mod_block, lambda i: (i // tiles_per_seq, 0, 0)),
            pl.BlockSpec(mod_block, lambda i: (nxt(i) // tiles_per_seq, 0, 0)),
            _resident((1, d)),
            _resident(wgu.shape),
            _resident(wd.shape),
        ] + cast_specs,
        out_specs=[row_spec] + cast_specs,
        out_shape=[jax.ShapeDtypeStruct((t, d), _f32)]
        + [jax.ShapeDtypeStruct(w.shape, _bf16) for w in later_weights],
        scratch_shapes=[pltpu.VMEM((step, d), _f32), pltpu.VMEM((step, d), _bf16),
                        pltpu.VMEM((step, d), _bf16)],
        compiler_params=pltpu.CompilerParams(
            dimension_semantics=("arbitrary",), vmem_limit_bytes=VMEM_LIMIT),
        name="ffn1",
    )(_runtime_zero(), h, h, mod, mod, norm_g, wgu, wd, *later_weights)
    return outs[0], outs[1:]


def _mixproj_kernel(tiles_per_seq, zero_ref, h0_ref, hn_ref, mod_ref, modn_ref, g_ref, w_ref, bm_ref,
                    cw_ref, wco_ref, q_ref, k_ref, v_ref, p1_ref, sb_ref, vbuf_ref, u_even_ref, u_odd_ref):
    i = pl.program_id(0)
    rows = ROW_TILE
    never = zero_ref[0] != 0

    @pl.when(i == 0)
    def _():
        u_even_ref[...] = _norm_mod(h0_ref[...], _gain(g_ref, mod_ref, 1), _shift(mod_ref, 1)).astype(_bf16)

    @pl.when(i % tiles_per_seq == 0)
    def _():
        vbuf_ref[0:SUBLANES, :] = jnp.zeros((SUBLANES, CONV_WIDTH), _f32)

    def step(u_ref, un_ref):
        jobs = _next_norm_jobs(never, hn_ref, un_ref, _gain(g_ref, modn_ref, 1), _shift(modn_ref, 1))
        per_dot = len(jobs) // 4
        u = u_ref[...]
        qkv0 = 3 * CONV_WIDTH
        g0 = qkv0 + 3 * ATTN_WIDTH
        def proj(col, width):
            return _dot(u, w_ref[:, col:col + width])

        cc = proj(CONV_WIDTH, CONV_WIDTH)
        token = _run_jobs(_token(cc), jobs[0:per_dot])
        cv = cc * _tie(never, token, proj(2 * CONV_WIDTH, CONV_WIDTH))
        vbuf_ref[SUBLANES:SUBLANES + rows, :] = cv
        y = (cw_ref[0:1, :] * vbuf_ref[SUBLANES - 2:SUBLANES - 2 + rows, :]
             + cw_ref[1:2, :] * vbuf_ref[SUBLANES - 1:SUBLANES - 1 + rows, :]
             + cw_ref[2:3, :] * cv)
        vbuf_ref[0:SUBLANES, :] = cv[rows - SUBLANES:, :]
        token = _run_jobs(_token(cv), jobs[per_dot:2 * per_dot])
        cb = _tie(never, token, proj(0, CONV_WIDTH))
        conv = (cb * y).astype(_bf16)

        q_ref[...] = (proj(qkv0, ATTN_WIDTH) * (HEAD_DIM ** -0.5)).astype(_bf16)
        k = proj(qkv0 + ATTN_WIDTH, ATTN_WIDTH)
        token = _run_jobs(_token(k), jobs[2 * per_dot:3 * per_dot])
        k_ref[...] = k.astype(_bf16)
        v_ref[...] = _tie(never, token, proj(qkv0 + 2 * ATTN_WIDTH, ATTN_WIDTH)).astype(_bf16)

        pga = proj(g0, D_MODEL)
        token = _run_jobs(_token(pga), jobs[3 * per_dot:])
        sa = jax.nn.sigmoid(pga + bm_ref[0:1, :])
        pgb = _tie(never, token, proj(g0 + D_MODEL, D_MODEL))
        sb_ref[...] = jax.nn.sigmoid(pgb + bm_ref[1:2, :]).astype(_bf16)
        p1_ref[...] = (sa * _dot(conv, wco_ref[...])).astype(_bf16)

    @pl.when(i % 2 == 0)
    def _():
        step(u_even_ref, u_odd_ref)

    @pl.when(i % 2 == 1)
    def _():
        step(u_odd_ref, u_even_ref)


def _mixproj(h, mod, norm_g, w_mix, b_merge, conv_w, w_conv_out, seq):
    t, d = h.shape
    tiles_per_seq = seq // ROW_TILE
    n_steps = t // ROW_TILE
    nxt = lambda i: jnp.minimum(i + 1, n_steps - 1)
    row = lambda width: pl.BlockSpec((ROW_TILE, width), lambda i: (i, 0))
    mod_block = (1, N_SUBLAYERS * N_MOD, d)
    return pl.pallas_call(
        functools.partial(_mixproj_kernel, tiles_per_seq),
        grid=(n_steps,),
        in_specs=[
            pl.BlockSpec(memory_space=pltpu.SMEM),
            pl.BlockSpec((ROW_TILE, d), lambda i: (0, 0)),
            pl.BlockSpec((ROW_TILE, d), lambda i: (nxt(i), 0)),
            pl.BlockSpec(mod_block, lambda i: (i // tiles_per_seq, 0, 0)),
            pl.BlockSpec(mod_block, lambda i: (nxt(i) // tiles_per_seq, 0, 0)),
            _resident((1, d)),
            _resident(w_mix.shape),
            _resident(b_merge.shape),
            _resident(conv_w.shape),
            _resident(w_conv_out.shape),
        ],
        out_specs=[row(ATTN_WIDTH), row(ATTN_WIDTH), row(ATTN_WIDTH), row(d), row(d)],
        out_shape=[jax.ShapeDtypeStruct((t, ATTN_WIDTH), _bf16)] * 3
        + [jax.ShapeDtypeStruct((t, d), _bf16)] * 2,
        scratch_shapes=[pltpu.VMEM((SUBLANES + ROW_TILE, CONV_WIDTH), _f32),
                        pltpu.VMEM((ROW_TILE, d), _bf16), pltpu.VMEM((ROW_TILE, d), _bf16)],
        compiler_params=pltpu.CompilerParams(
            dimension_semantics=("arbitrary",), vmem_limit_bytes=VMEM_LIMIT),
        name="mixproj",
    )(_runtime_zero(), h, h, mod, mod, norm_g, w_mix, b_merge, conv_w, w_conv_out)


def _attn_kernel(q_ref, k_ref, v_ref, tri_ref, o_ref, acc_ref, rem_ref):
    step = pl.program_id(1)
    bq = ATTN_BLOCK
    n_pairs = q_ref.shape[2] // LANES
    lane = lax.broadcasted_iota(jnp.int32, (bq, LANES), 1).astype(_f32).astype(_bf16)
    first_head = lane < HEAD_DIM

    def split_heads(x):
        zero = jnp.zeros_like(x)
        return jnp.concatenate([jnp.where(first_head, x, zero), jnp.where(first_head, zero, x)], axis=0)

    row = lax.broadcasted_iota(jnp.int32, (2 * bq, bq), 0)
    col = lax.broadcasted_iota(jnp.int32, (2 * bq, bq), 1)
    strictly_causal = col < jnp.where(row >= bq, row - bq, row)

    pairs = range(n_pairs)
    lanes = [slice(p * LANES, (p + 1) * LANES) for p in pairs]
    q_rows = [slice(b * bq, (b + 1) * bq) for b in range(Q_BLOCKS_PER_STEP)]
    q2 = [[split_heads(q_ref[0, q_rows[b], lanes[p]]) for p in pairs] for b in range(Q_BLOCKS_PER_STEP)]

    def scores(b, p, start, n_blocks):
        return lax.dot_general(q2[b][p], k_ref[0, pl.ds(start, n_blocks * bq), lanes[p]],
                               (((1,), (1,)), ((), ())), preferred_element_type=_f32)

    def visit(plan, first):
        chains = []
        for b, blocks in plan:
            starts = [pl.multiple_of(j * bq, bq) for j, _ in blocks]
            if len(blocks) == 3:
                both = [scores(b, p, starts[2], 2) for p in pairs]
                zs = [[scores(b, p, starts[0], 1) for p in pairs],
                      [z[:, bq:] for z in both], [z[:, :bq] for z in both]]
            else:
                zs = [[scores(b, p, start, 1) for p in pairs] for start in starts]
            chains += [(b, start, diagonal, z) for start, (_, diagonal), z in zip(starts, blocks, zs)]
        sps = []
        for _, _, diagonal, z_block in chains:
            row_sp = []
            for z in z_block:
                sp = jnp.maximum(z, 0.0) + jnp.log(1.0 + jnp.exp(-jnp.abs(z)))
                if diagonal:
                    sp = jnp.where(strictly_causal, sp, 0.0)
                row_sp.append(sp.astype(_bf16))
            sps.append(row_sp)
        css = [[_dot(x, tri_ref[...]) for x in row_sp] for row_sp in sps]
        seen = {b: [None if first else rem_ref[b, p] for p in pairs] for b, _ in plan}
        weights = []
        for n, (b, _, diagonal, z_block) in enumerate(chains):
            row_w = []
            for p in pairs:
                log_a = z_block[p] - css[n][p][:, :bq]
                a = jnp.exp(log_a if seen[b][p] is None else log_a - seen[b][p])
                if diagonal:
                    a = jnp.where(strictly_causal, a, 0.0)
                a = a.astype(_bf16)
                row_w.append(jnp.concatenate([a[:bq], a[bq:]], axis=1))
                total = css[n][p][:, bq:]
                seen[b][p] = total if seen[b][p] is None else seen[b][p] + total
            weights.append(row_w)
        for b, _ in plan:
            for p in pairs:
                out = None
                for n, (cb, start, _, _) in enumerate(chains):
                    if cb != b:
                        continue
                    term = _dot(weights[n][p], split_heads(v_ref[0, pl.ds(start, bq), lanes[p]]))
                    out = term if out is None else out + term
                if first:
                    acc_ref[b, p] = out
                else:
                    acc_ref[b, p] += out
                rem_ref[b, p] = seen[b][p]

    first_block = step * Q_BLOCKS_PER_STEP
    fused = step >= 1

    @pl.when(fused)
    def _():
        visit([(b, [(first_block + b, True), (first_block + b - 1, False), (first_block + b - 2, False)])
               for b in reversed(range(Q_BLOCKS_PER_STEP))], first=True)

    @pl.when(jnp.logical_not(fused))
    def _():
        visit([(b, [(b, True)] + [(j, False) for j in reversed(range(b))])
               for b in reversed(range(Q_BLOCKS_PER_STEP))], first=True)

    def unfinished(b):
        return jnp.min(rem_ref[b]) < SKIP_LOG

    live = [unfinished(b) for b in range(Q_BLOCKS_PER_STEP)]
    for b in range(Q_BLOCKS_PER_STEP):
        def cond(state):
            j, go = state
            return jnp.logical_and(j >= 0, go)

        def body(state, b=b):
            j, _ = state
            visit([(b, [(j, False)])], first=False)
            return j - 1, unfinished(b)

        lax.while_loop(cond, body, (jnp.where(fused, first_block + b - 3, -1), live[b]))
    for b in range(Q_BLOCKS_PER_STEP):
        for p in pairs:
            o_ref[0, q_rows[b], lanes[p]] = acc_ref[b, p].astype(o_ref.dtype)


def _tri_matrix():
    bq = ATTN_BLOCK
    j = lax.broadcasted_iota(jnp.int32, (bq, bq), 0)
    s = lax.broadcasted_iota(jnp.int32, (bq, bq), 1)
    return jnp.concatenate([(j >= s).astype(_bf16), jnp.ones((bq, bq), _bf16)], axis=1)


def _attention(q, k, v):
    batch, seq, width = q.shape
    n_pairs = width // LANES
    bq = ATTN_BLOCK
    rows = Q_BLOCKS_PER_STEP * bq
    blk = pl.BlockSpec((1, rows, width), lambda b, i: (b, i, 0))
    whole = pl.BlockSpec((1, seq, width), lambda b, i: (b, 0, 0))
    return pl.pallas_call(
        _attn_kernel,
        grid=(batch, seq // rows),
        in_specs=[blk, whole, whole, _resident((bq, 2 * bq))],
        out_specs=blk,
        out_shape=jax.ShapeDtypeStruct((batch, seq, width), _bf16),
        scratch_shapes=[pltpu.VMEM((Q_BLOCKS_PER_STEP, n_pairs, bq, LANES), _f32),
                        pltpu.VMEM((Q_BLOCKS_PER_STEP, n_pairs, 2 * bq, bq), _f32)],
        compiler_params=pltpu.CompilerParams(
            dimension_semantics=("arbitrary", "arbitrary"), vmem_limit_bytes=VMEM_LIMIT),
        name="attn",
    )(q, k, v, _tri_matrix())


def _tail_kernel(o_ref, p1_ref, sb_ref, h_ref, mod_ref, wao_ref, wo_ref, g3_ref, wgu_ref, wd_ref,
                 gf_ref, out_ref, acc_ref):
    halves = [slice(r, r + TAIL_ROWS // 2) for r in (0, TAIL_ROWS // 2)]
    yb = [_dot(o_ref[rows, :], wao_ref[...]) for rows in halves]
    merged = [(p1_ref[rows, :].astype(_f32) + sb_ref[rows, :].astype(_f32) * yb_h).astype(_bf16)
              for rows, yb_h in zip(halves, yb)]
    y = [_dot(m, wo_ref[...]) for m in merged]
    h2 = jnp.concatenate([h_ref[rows, :] + _gate(mod_ref, 1) * y_h for rows, y_h in zip(halves, y)], axis=0)
    u = _norm_mod(h2, _gain(g3_ref, mod_ref, 2), _shift(mod_ref, 2)).astype(_bf16)
    _swiglu_into(acc_ref, u, wgu_ref, wd_ref)
    h3 = h2 + (FFN_RES_WEIGHT * _gate(mod_ref, 2)) * acc_ref[...]
    inv = lax.rsqrt(jnp.mean(h3 * h3, axis=-1, keepdims=True) + EPS)
    out_ref[...] = (h3 * inv) * gf_ref[...]


def _tail(o, p1, sb, h, mod, w_attn_out, w_out, norm_g, wgu, wd, final_g, seq):
    t, d = h.shape
    tiles_per_seq = seq // TAIL_ROWS
    row = lambda width: pl.BlockSpec((TAIL_ROWS, width), lambda i: (i, 0))
    return pl.pallas_call(
        _tail_kernel,
        grid=(t // TAIL_ROWS,),
        in_specs=[
            row(ATTN_WIDTH), row(d), row(d), row(d),
            pl.BlockSpec((1, N_SUBLAYERS * N_MOD, d), lambda i: (i // tiles_per_seq, 0, 0)),
            _resident(w_attn_out.shape),
            _resident(w_out.shape),
            _resident((1, d)),
            _resident(wgu.shape),
            _resident(wd.shape),
            _resident((1, d)),
        ],
        out_specs=row(d),
        out_shape=jax.ShapeDtypeStruct((t, d), _f32),
        scratch_shapes=[pltpu.VMEM((TAIL_ROWS, d), _f32)],
        compiler_params=pltpu.CompilerParams(
            dimension_semantics=("arbitrary",), vmem_limit_bytes=TAIL_VMEM_LIMIT),
        name="tail",
    )(o, p1, sb, h, mod, w_attn_out, w_out, norm_g, wgu, wd, final_g)


def kernel(x, c, w_ada, b_ada, norm1_g, ffn1_w_gu, ffn1_w_down, norm2_g, w_mix_in, b_merge, conv_w,
           w_conv_out, w_attn_out, w_out, norm3_g, ffn2_w_gu, ffn2_w_down, final_g):
    batch, seq, d = x.shape
    assert w_ada.shape[0] == 1
    assert seq % ROW_TILE == 0 and seq % TAIL_ROWS == 0 and d == D_MODEL
    assert seq % (Q_BLOCKS_PER_STEP * ATTN_BLOCK) == 0
    assert w_mix_in.shape[1:] == (d, MIX_IN_WIDTH) and conv_w.shape[1:] == (CONV_KSIZE, CONV_WIDTH)
    assert ROW_TILE // PIECE_ROWS <= N_FF_CHUNKS and (ROW_TILE // PIECE_ROWS) % 4 == 0
    h = x.reshape(batch * seq, d)
    mod, (w_gu1, w_d1) = _adaln(c, w_ada[0], b_ada[0], [ffn1_w_gu[0], ffn1_w_down[0]])
    mod = mod.reshape(batch, N_SUBLAYERS * N_MOD, d)
    h1, (w_mix, w_co, w_ao, w_o, w_gu2, w_d2) = _ffn1(
        h, mod, norm1_g[0].reshape(1, d), w_gu1, w_d1,
        [w_mix_in[0], w_conv_out[0], w_attn_out[0], w_out[0], ffn2_w_gu[0], ffn2_w_down[0]], seq)
    q, k, v, p1, sb = _mixproj(h1, mod, norm2_g[0].reshape(1, d), w_mix, b_merge[0], conv_w[0],
                               w_co, seq)
    o = _attention(q.reshape(batch, seq, ATTN_WIDTH), k.reshape(batch, seq, ATTN_WIDTH),
                   v.reshape(batch, seq, ATTN_WIDTH)).reshape(batch * seq, ATTN_WIDTH)
    out = _tail(o, p1, sb, h1, mod, w_ao, w_o, norm3_g[0].reshape(1, d), w_gu2, w_d2,
                final_g.reshape(1, d), seq)
    return out.reshape(batch, seq, d)
```

```python
import functools

import jax
import jax.numpy as jnp
from jax import lax
from jax.experimental import pallas as pl
from jax.experimental.pallas import tpu as pltpu

D_MODEL = 1024
N_HEADS = 8
HEAD_DIM = 64
ATTN_WIDTH = N_HEADS * HEAD_DIM
CONV_WIDTH = 512
CONV_KSIZE = 3
D_FF = 2816
N_SUBLAYERS = 3
N_MOD = 3
EPS = 1e-6
FFN_RES_WEIGHT = 0.5
MIX_IN_WIDTH = 3 * CONV_WIDTH + 3 * ATTN_WIDTH + 2 * D_MODEL

LANES = 128
SUBLANES = 8
BF16_SUBLANES = 16
MXU_DIM = 256
VMEM_LIMIT = 56 * 1024 * 1024

ADALN_COLS = 1152
ROW_TILE = 512
MIX_ROWS = 1024
TAIL_ROWS = 1024
TAIL_VMEM_LIMIT = 60 * 1024 * 1024
PIECE_ROWS = 64
FF_CHUNK = MXU_DIM
N_FF_CHUNKS = D_FF // FF_CHUNK
DOWN_GROUP = 3
ATTN_BLOCK = 128
Q_BLOCKS_PER_STEP = 4
SKIP_LOG = 106.0

assert D_FF % FF_CHUNK == 0

_bf16 = jnp.bfloat16
_f32 = jnp.float32


def _runtime_zero():
    return jnp.zeros((1,), jnp.int32)


def _dot(a, b):
    return jnp.dot(a, b, preferred_element_type=_f32)


def _split_bf16(x):
    hi = x.astype(_bf16)
    return hi, (x - hi.astype(_f32)).astype(_bf16)


def _resident(shape):
    zeros = (0,) * len(shape)
    return pl.BlockSpec(shape, lambda *_: zeros, pipeline_mode=pl.Buffered(1))


def _norm_mod(x, gain, shift):
    inv = lax.rsqrt(jnp.mean(x * x, axis=-1, keepdims=True) + EPS)
    return (x * inv) * gain + shift


def _gain(g_ref, mod_ref, sublayer):
    return g_ref[...] * (1.0 + mod_ref[0, N_MOD * sublayer + 1:N_MOD * sublayer + 2, :])


def _shift(mod_ref, sublayer):
    return mod_ref[0, N_MOD * sublayer:N_MOD * sublayer + 1, :]


def _gate(mod_ref, sublayer):
    return mod_ref[0, N_MOD * sublayer + 2:N_MOD * sublayer + 3, :]


def _token(value):
    return value[0:1, 0:LANES].astype(_f32)


def _tie(never, token, value):
    head = jnp.where(never, token, value[:, :LANES])
    if value.shape[1] == LANES:
        return head
    return jnp.concatenate([head, value[:, LANES:]], axis=1)


def _run_jobs(token, jobs):
    for job in jobs:
        token = job(token)
    return token


def _next_norm_jobs(never, hn_ref, un_ref, gain, shift):
    def job(k):
        rows = slice(k * PIECE_ROWS, (k + 1) * PIECE_ROWS)

        def run(token):
            piece = _norm_mod(hn_ref[rows, :], _tie(never, token, gain), shift).astype(_bf16)
            un_ref[rows, :] = piece
            return _token(piece)
        return run
    return [job(k) for k in range(hn_ref.shape[0] // PIECE_ROWS)]


def _swiglu_into(acc_ref, u_bf16, wgu_ref, wd_ref, never=None, side_jobs=()):
    pending = {}
    acts = []
    for c in range(N_FF_CHUNKS):
        cols = slice(c * FF_CHUNK, (c + 1) * FF_CHUNK)
        up_cols = slice(D_FF + c * FF_CHUNK, D_FF + (c + 1) * FF_CHUNK)
        g = _dot(u_bf16, wgu_ref[:, cols])
        if c - 2 in pending:
            g = _tie(never, pending.pop(c - 2), g)
        if c < len(side_jobs) and side_jobs[c]:
            pending[c] = _run_jobs(_token(g), side_jobs[c])
        up = _dot(u_bf16, wgu_ref[:, up_cols])
        acts.append((g * jax.nn.sigmoid(g) * up).astype(_bf16))
        if len(acts) == DOWN_GROUP or c == N_FF_CHUNKS - 1:
            first_col = (c + 1 - len(acts)) * FF_CHUNK
            act = acts[0] if len(acts) == 1 else jnp.concatenate(acts, axis=1)
            down = _dot(act, wd_ref[first_col:(c + 1) * FF_CHUNK, :])
            if first_col == 0:
                acc_ref[...] = down
            else:
                acc_ref[...] += down
            acts = []
    return list(pending.values())


def _adaln_kernel(n_cast, c_ref, w_ref, b_ref, *rest):
    cast_in = rest[:n_cast]
    o_ref = rest[n_cast]
    cast_out = rest[n_cast + 1:]
    c = c_ref[...]
    c_act = c * jax.nn.sigmoid(c)
    c_hi, c_lo = _split_bf16(c_act)
    w_hi, w_lo = _split_bf16(w_ref[...])
    o_ref[...] = _dot(c_hi, w_hi) + (_dot(c_lo, w_hi) + _dot(c_hi, w_lo)) + b_ref[...]
    for src, dst in zip(cast_in, cast_out):
        dst[...] = src[...].astype(_bf16)


def _cast_slab_spec(shape, n_steps):
    rows, cols = shape
    span = 1
    while (rows * span) % (n_steps * BF16_SUBLANES) != 0:
        span *= 2
    return pl.BlockSpec((rows * span // n_steps, cols), lambda i: (i // span, 0))


def _adaln(c, w_ada, b_ada, cast_weights):
    batch, d = c.shape
    n = w_ada.shape[1]
    n_steps = n // ADALN_COLS
    cast_specs = [_cast_slab_spec(w.shape, n_steps) for w in cast_weights]
    outs = pl.pallas_call(
        functools.partial(_adaln_kernel, len(cast_weights)),
        grid=(n_steps,),
        in_specs=[
            pl.BlockSpec((batch, d), lambda j: (0, 0)),
            pl.BlockSpec((d, ADALN_COLS), lambda j: (0, j)),
            pl.BlockSpec((1, ADALN_COLS), lambda j: (0, j)),
        ] + cast_specs,
        out_specs=[pl.BlockSpec((batch, ADALN_COLS), lambda j: (0, j))] + cast_specs,
        out_shape=[jax.ShapeDtypeStruct((batch, n), _f32)]
        + [jax.ShapeDtypeStruct(w.shape, _bf16) for w in cast_weights],
        compiler_params=pltpu.CompilerParams(
            dimension_semantics=("arbitrary",), vmem_limit_bytes=VMEM_LIMIT),
        name="adaln",
    )(c, w_ada, b_ada.reshape(1, n), *cast_weights)
    return outs[0], outs[1:]


def _ffn1_kernel(n_cast, zero_ref, h_ref, hn_ref, mod_ref, modn_ref, g_ref, wgu_ref, wd_ref, *rest):
    cast_in = rest[:n_cast]
    o_ref = rest[n_cast]
    cast_out = rest[n_cast + 1:2 * n_cast + 1]
    acc_ref, u_even_ref, u_odd_ref = rest[2 * n_cast + 1:]
    i = pl.program_id(0)
    never = zero_ref[0] != 0

    @pl.when(i == 0)
    def _():
        u_even_ref[...] = _norm_mod(h_ref[...], _gain(g_ref, mod_ref, 0), _shift(mod_ref, 0)).astype(_bf16)

    def cast_job(src, dst):
        def run(token):
            w = _tie(never, token, src[...]).astype(_bf16)
            dst[...] = w
            return _token(w)
        return run

    def step(u_ref, un_ref):
        norm_jobs = _next_norm_jobs(never, hn_ref, un_ref, _gain(g_ref, modn_ref, 0), _shift(modn_ref, 0))
        cast_jobs = [cast_job(src, dst) for src, dst in zip(cast_in, cast_out)]
        n_tail = N_FF_CHUNKS - len(norm_jobs)
        side_jobs = [[job] for job in norm_jobs] + [cast_jobs[r::n_tail] for r in range(n_tail)]
        tokens = _swiglu_into(acc_ref, u_ref[...], wgu_ref, wd_ref, never, side_jobs)
        acc = acc_ref[...]
        for token in tokens:
            acc = _tie(never, token, acc)
        o_ref[...] = h_ref[...] + (FFN_RES_WEIGHT * _gate(mod_ref, 0)) * acc

    @pl.when(i % 2 == 0)
    def _():
        step(u_even_ref, u_odd_ref)

    @pl.when(i % 2 == 1)
    def _():
        step(u_odd_ref, u_even_ref)


def _ffn1(h, mod, norm_g, wgu, wd, later_weights, seq):
    t, d = h.shape
    step = ROW_TILE
    tiles_per_seq = seq // step
    n_steps = t // step
    cast_specs = [_cast_slab_spec(w.shape, n_steps) for w in later_weights]
    row_spec = pl.BlockSpec((step, d), lambda i: (i, 0))
    nxt = lambda i: jnp.minimum(i + 1, n_steps - 1)
    mod_block = (1, N_SUBLAYERS * N_MOD, d)
    outs = pl.pallas_call(
        functools.partial(_ffn1_kernel, len(later_weights)),
        grid=(n_steps,),
        in_specs=[
            pl.BlockSpec(memory_space=pltpu.SMEM),
            row_spec,
            pl.BlockSpec((step, d), lambda i: (nxt(i), 0)),
            pl.BlockSpec(mod_block, lambda i: (i // tiles_per_seq, 0, 0)),
            pl.BlockSpec(mod_block, lambda i: (nxt(i) // tiles_per_seq, 0, 0)),
            _resident((1, d)),
            _resident(wgu.shape),
            _resident(wd.shape),
        ] + cast_specs,
        out_specs=[row_spec] + cast_specs,
        out_shape=[jax.ShapeDtypeStruct((t, d), _f32)]
        + [jax.ShapeDtypeStruct(w.shape, _bf16) for w in later_weights],
        scratch_shapes=[pltpu.VMEM((step, d), _f32), pltpu.VMEM((step, d), _bf16),
                        pltpu.VMEM((step, d), _bf16)],
        compiler_params=pltpu.CompilerParams(
            dimension_semantics=("arbitrary",), vmem_limit_bytes=VMEM_LIMIT),
        name="ffn1",
    )(_runtime_zero(), h, h, mod, mod, norm_g, wgu, wd, *later_weights)
    return outs[0], outs[1:]


def _mixproj_kernel(tiles_per_seq, zero_ref, h0_ref, hn_ref, mod_ref, modn_ref, g_ref, w_ref, bm_ref,
                    cw_ref, wco_ref, q_ref, k_ref, v_ref, p1_ref, sb_ref, vbuf_ref, u_even_ref, u_odd_ref):
    i = pl.program_id(0)
    rows = MIX_ROWS
    never = zero_ref[0] != 0

    @pl.when(i == 0)
    def _():
        u_even_ref[...] = _norm_mod(h0_ref[...], _gain(g_ref, mod_ref, 1), _shift(mod_ref, 1)).astype(_bf16)

    @pl.when(i % tiles_per_seq == 0)
    def _():
        vbuf_ref[0:SUBLANES, :] = jnp.zeros((SUBLANES, CONV_WIDTH), _f32)

    def step(u_ref, un_ref):
        jobs = _next_norm_jobs(never, hn_ref, un_ref, _gain(g_ref, modn_ref, 1), _shift(modn_ref, 1))
        per_dot = len(jobs) // 4
        u = u_ref[...]
        qkv0 = 3 * CONV_WIDTH
        g0 = qkv0 + 3 * ATTN_WIDTH
        def proj(col, width):
            return _dot(u, w_ref[:, col:col + width])

        cc = proj(CONV_WIDTH, CONV_WIDTH)
        token = _run_jobs(_token(cc), jobs[0:per_dot])
        cv = cc * _tie(never, token, proj(2 * CONV_WIDTH, CONV_WIDTH))
        vbuf_ref[SUBLANES:SUBLANES + rows, :] = cv
        y = (cw_ref[0:1, :] * vbuf_ref[SUBLANES - 2:SUBLANES - 2 + rows, :]
             + cw_ref[1:2, :] * vbuf_ref[SUBLANES - 1:SUBLANES - 1 + rows, :]
             + cw_ref[2:3, :] * cv)
        vbuf_ref[0:SUBLANES, :] = cv[rows - SUBLANES:, :]
        token = _run_jobs(_token(cv), jobs[per_dot:2 * per_dot])
        cb = _tie(never, token, proj(0, CONV_WIDTH))
        conv = (cb * y).astype(_bf16)

        q_ref[...] = (proj(qkv0, ATTN_WIDTH) * (HEAD_DIM ** -0.5)).astype(_bf16)
        k = proj(qkv0 + ATTN_WIDTH, ATTN_WIDTH)
        token = _run_jobs(_token(k), jobs[2 * per_dot:3 * per_dot])
        k_ref[...] = k.astype(_bf16)
        v_ref[...] = _tie(never, token, proj(qkv0 + 2 * ATTN_WIDTH, ATTN_WIDTH)).astype(_bf16)

        pga = proj(g0, D_MODEL)
        token = _run_jobs(_token(pga), jobs[3 * per_dot:])
        sa = jax.nn.sigmoid(pga + bm_ref[0:1, :])
        pgb = _tie(never, token, proj(g0 + D_MODEL, D_MODEL))
        sb_ref[...] = jax.nn.sigmoid(pgb + bm_ref[1:2, :]).astype(_bf16)
        p1_ref[...] = (sa * _dot(conv, wco_ref[...])).astype(_bf16)

    @pl.when(i % 2 == 0)
    def _():
        step(u_even_ref, u_odd_ref)

    @pl.when(i % 2 == 1)
    def _():
        step(u_odd_ref, u_even_ref)


def _mixproj(h, mod, norm_g, w_mix, b_merge, conv_w, w_conv_out, seq):
    t, d = h.shape
    tiles_per_seq = seq // MIX_ROWS
    n_steps = t // MIX_ROWS
    nxt = lambda i: jnp.minimum(i + 1, n_steps - 1)
    row = lambda width: pl.BlockSpec((MIX_ROWS, width), lambda i: (i, 0))
    mod_block = (1, N_SUBLAYERS * N_MOD, d)
    return pl.pallas_call(
        functools.partial(_mixproj_kernel, tiles_per_seq),
        grid=(n_steps,),
        in_specs=[
            pl.BlockSpec(memory_space=pltpu.SMEM),
            pl.BlockSpec((MIX_ROWS, d), lambda i: (0, 0)),
            pl.BlockSpec((MIX_ROWS, d), lambda i: (nxt(i), 0)),
            pl.BlockSpec(mod_block, lambda i: (i // tiles_per_seq, 0, 0)),
            pl.BlockSpec(mod_block, lambda i: (nxt(i) // tiles_per_seq, 0, 0)),
            _resident((1, d)),
            _resident(w_mix.shape),
            _resident(b_merge.shape),
            _resident(conv_w.shape),
            _resident(w_conv_out.shape),
        ],
        out_specs=[row(ATTN_WIDTH), row(ATTN_WIDTH), row(ATTN_WIDTH), row(d), row(d)],
        out_shape=[jax.ShapeDtypeStruct((t, ATTN_WIDTH), _bf16)] * 3
        + [jax.ShapeDtypeStruct((t, d), _bf16)] * 2,
        scratch_shapes=[pltpu.VMEM((SUBLANES + MIX_ROWS, CONV_WIDTH), _f32),
                        pltpu.VMEM((MIX_ROWS, d), _bf16), pltpu.VMEM((MIX_ROWS, d), _bf16)],
        compiler_params=pltpu.CompilerParams(
            dimension_semantics=("arbitrary",), vmem_limit_bytes=TAIL_VMEM_LIMIT),
        name="mixproj",
    )(_runtime_zero(), h, h, mod, mod, norm_g, w_mix, b_merge, conv_w, w_conv_out)


def _attn_kernel(q_ref, k_ref, v_ref, tri_ref, o_ref, acc_ref, rem_ref):
    step = pl.program_id(1)
    bq = ATTN_BLOCK
    n_pairs = q_ref.shape[2] // LANES
    lane = lax.broadcasted_iota(jnp.int32, (bq, LANES), 1).astype(_f32).astype(_bf16)
    first_head = lane < HEAD_DIM

    def split_heads(x):
        zero = jnp.zeros_like(x)
        return jnp.concatenate([jnp.where(first_head, x, zero), jnp.where(first_head, zero, x)], axis=0)

    row = lax.broadcasted_iota(jnp.int32, (2 * bq, bq), 0)
    col = lax.broadcasted_iota(jnp.int32, (2 * bq, bq), 1)
    strictly_causal = col < jnp.where(row >= bq, row - bq, row)

    pairs = range(n_pairs)
    lanes = [slice(p * LANES, (p + 1) * LANES) for p in pairs]
    q_rows = [slice(b * bq, (b + 1) * bq) for b in range(Q_BLOCKS_PER_STEP)]
    q2 = [[split_heads(q_ref[0, q_rows[b], lanes[p]]) for p in pairs] for b in range(Q_BLOCKS_PER_STEP)]

    def scores(b, p, start, n_blocks):
        return lax.dot_general(q2[b][p], k_ref[0, pl.ds(start, n_blocks * bq), lanes[p]],
                               (((1,), (1,)), ((), ())), preferred_element_type=_f32)

    def visit(plan, first):
        chains = []
        for b, blocks in plan:
            starts = [pl.multiple_of(j * bq, bq) for j, _ in blocks]
            if len(blocks) == 3:
                both = [scores(b, p, starts[2], 2) for p in pairs]
                zs = [[scores(b, p, starts[0], 1) for p in pairs],
                      [z[:, bq:] for z in both], [z[:, :bq] for z in both]]
            else:
                zs = [[scores(b, p, start, 1) for p in pairs] for start in starts]
            chains += [(b, start, diagonal, z) for start, (_, diagonal), z in zip(starts, blocks, zs)]
        sps = []
        for _, _, diagonal, z_block in chains:
            row_sp = []
            for z in z_block:
                sp = jnp.maximum(z, 0.0) + jnp.log(1.0 + jnp.exp(-jnp.abs(z)))
                if diagonal:
                    sp = jnp.where(strictly_causal, sp, 0.0)
                row_sp.append(sp.astype(_bf16))
            sps.append(row_sp)
        css = [[_dot(x, tri_ref[...]) for x in row_sp] for row_sp in sps]
        seen = {b: [None if first else rem_ref[b, p] for p in pairs] for b, _ in plan}
        weights = []
        for n, (b, _, diagonal, z_block) in enumerate(chains):
            row_w = []
            for p in pairs:
                log_a = z_block[p] - css[n][p][:, :bq]
                a = jnp.exp(log_a if seen[b][p] is None else log_a - seen[b][p])
                if diagonal:
                    a = jnp.where(strictly_causal, a, 0.0)
                a = a.astype(_bf16)
                row_w.append(jnp.concatenate([a[:bq], a[bq:]], axis=1))
                total = css[n][p][:, bq:]
                seen[b][p] = total if seen[b][p] is None else seen[b][p] + total
            weights.append(row_w)
        for b, _ in plan:
            for p in pairs:
                out = None
                for n, (cb, start, _, _) in enumerate(chains):
                    if cb != b:
                        continue
                    term = _dot(weights[n][p], split_heads(v_ref[0, pl.ds(start, bq), lanes[p]]))
                    out = term if out is None else out + term
                if first:
                    acc_ref[b, p] = out
                else:
                    acc_ref[b, p] += out
                rem_ref[b, p] = seen[b][p]

    first_block = step * Q_BLOCKS_PER_STEP
    fused = step >= 1

    @pl.when(fused)
    def _():
        visit([(b, [(first_block + b, True), (first_block + b - 1, False), (first_block + b - 2, False)])
               for b in reversed(range(Q_BLOCKS_PER_STEP))], first=True)

    @pl.when(jnp.logical_not(fused))
    def _():
        visit([(b, [(b, True)] + [(j, False) for j in reversed(range(b))])
               for b in reversed(range(Q_BLOCKS_PER_STEP))], first=True)

    def unfinished(b):
        return jnp.min(rem_ref[b]) < SKIP_LOG

    live = [unfinished(b) for b in range(Q_BLOCKS_PER_STEP)]
    for b in range(Q_BLOCKS_PER_STEP):
        def cond(state):
            j, go = state
            return jnp.logical_and(j >= 0, go)

        def body(state, b=b):
            j, _ = state
            visit([(b, [(j, False)])], first=False)
            return j - 1, unfinished(b)

        lax.while_loop(cond, body, (jnp.where(fused, first_block + b - 3, -1), live[b]))
    for b in range(Q_BLOCKS_PER_STEP):
        for p in pairs:
            o_ref[0, q_rows[b], lanes[p]] = acc_ref[b, p].astype(o_ref.dtype)


def _tri_matrix():
    bq = ATTN_BLOCK
    j = lax.broadcasted_iota(jnp.int32, (bq, bq), 0)
    s = lax.broadcasted_iota(jnp.int32, (bq, bq), 1)
    return jnp.concatenate([(j >= s).astype(_bf16), jnp.ones((bq, bq), _bf16)], axis=1)


def _attention(q, k, v):
    batch, seq, width = q.shape
    n_pairs = width // LANES
    bq = ATTN_BLOCK
    rows = Q_BLOCKS_PER_STEP * bq
    blk = pl.BlockSpec((1, rows, width), lambda b, i: (b, i, 0))
    whole = pl.BlockSpec((1, seq, width), lambda b, i: (b, 0, 0))
    return pl.pallas_call(
        _attn_kernel,
        grid=(batch, seq // rows),
        in_specs=[blk, whole, whole, _resident((bq, 2 * bq))],
        out_specs=blk,
        out_shape=jax.ShapeDtypeStruct((batch, seq, width), _bf16),
        scratch_shapes=[pltpu.VMEM((Q_BLOCKS_PER_STEP, n_pairs, bq, LANES), _f32),
                        pltpu.VMEM((Q_BLOCKS_PER_STEP, n_pairs, 2 * bq, bq), _f32)],
        compiler_params=pltpu.CompilerParams(
            dimension_semantics=("arbitrary", "arbitrary"), vmem_limit_bytes=VMEM_LIMIT),
        name="attn",
    )(q, k, v, _tri_matrix())


def _tail_kernel(o_ref, p1_ref, sb_ref, h_ref, mod_ref, wao_ref, wo_ref, g3_ref, wgu_ref, wd_ref,
                 gf_ref, out_ref, acc_ref):
    halves = [slice(r, r + TAIL_ROWS // 2) for r in (0, TAIL_ROWS // 2)]
    yb = [_dot(o_ref[rows, :], wao_ref[...]) for rows in halves]
    merged = [(p1_ref[rows, :].astype(_f32) + sb_ref[rows, :].astype(_f32) * yb_h).astype(_bf16)
              for rows, yb_h in zip(halves, yb)]
    y = [_dot(m, wo_ref[...]) for m in merged]
    h2 = jnp.concatenate([h_ref[rows, :] + _gate(mod_ref, 1) * y_h for rows, y_h in zip(halves, y)], axis=0)
    u = _norm_mod(h2, _gain(g3_ref, mod_ref, 2), _shift(mod_ref, 2)).astype(_bf16)
    _swiglu_into(acc_ref, u, wgu_ref, wd_ref)
    h3 = h2 + (FFN_RES_WEIGHT * _gate(mod_ref, 2)) * acc_ref[...]
    inv = lax.rsqrt(jnp.mean(h3 * h3, axis=-1, keepdims=True) + EPS)
    out_ref[...] = (h3 * inv) * gf_ref[...]


def _tail(o, p1, sb, h, mod, w_attn_out, w_out, norm_g, wgu, wd, final_g, seq):
    t, d = h.shape
    tiles_per_seq = seq // TAIL_ROWS
    row = lambda width: pl.BlockSpec((TAIL_ROWS, width), lambda i: (i, 0))
    return pl.pallas_call(
        _tail_kernel,
        grid=(t // TAIL_ROWS,),
        in_specs=[
            row(ATTN_WIDTH), row(d), row(d), row(d),
            pl.BlockSpec((1, N_SUBLAYERS * N_MOD, d), lambda i: (i // tiles_per_seq, 0, 0)),
            _resident(w_attn_out.shape),
            _resident(w_out.shape),
            _resident((1, d)),
            _resident(wgu.shape),
            _resident(wd.shape),
            _resident((1, d)),
        ],
        out_specs=row(d),
        out_shape=jax.ShapeDtypeStruct((t, d), _f32),
        scratch_shapes=[pltpu.VMEM((TAIL_ROWS, d), _f32)],
        compiler_params=pltpu.CompilerParams(
            dimension_semantics=("arbitrary",), vmem_limit_bytes=TAIL_VMEM_LIMIT),
        name="tail",
    )(o, p1, sb, h, mod, w_attn_out, w_out, norm_g, wgu, wd, final_g)


def kernel(x, c, w_ada, b_ada, norm1_g, ffn1_w_gu, ffn1_w_down, norm2_g, w_mix_in, b_merge, conv_w,
           w_conv_out, w_attn_out, w_out, norm3_g, ffn2_w_gu, ffn2_w_down, final_g):
    batch, seq, d = x.shape
    assert w_ada.shape[0] == 1
    assert seq % ROW_TILE == 0 and seq % TAIL_ROWS == 0 and seq % MIX_ROWS == 0 and d == D_MODEL
    assert (MIX_ROWS // PIECE_ROWS) % 4 == 0
    assert seq % (Q_BLOCKS_PER_STEP * ATTN_BLOCK) == 0
    assert w_mix_in.shape[1:] == (d, MIX_IN_WIDTH) and conv_w.shape[1:] == (CONV_KSIZE, CONV_WIDTH)
    assert ROW_TILE // PIECE_ROWS <= N_FF_CHUNKS and (ROW_TILE // PIECE_ROWS) % 4 == 0
    h = x.reshape(batch * seq, d)
    mod, (w_gu1, w_d1) = _adaln(c, w_ada[0], b_ada[0], [ffn1_w_gu[0], ffn1_w_down[0]])
    mod = mod.reshape(batch, N_SUBLAYERS * N_MOD, d)
    h1, (w_mix, w_co, w_ao, w_o, w_gu2, w_d2) = _ffn1(
        h, mod, norm1_g[0].reshape(1, d), w_gu1, w_d1,
        [w_mix_in[0], w_conv_out[0], w_attn_out[0], w_out[0], ffn2_w_gu[0], ffn2_w_down[0]], seq)
    q, k, v, p1, sb = _mixproj(h1, mod, norm2_g[0].reshape(1, d), w_mix, b_merge[0], conv_w[0],
                               w_co, seq)
    o = _attention(q.reshape(batch, seq, ATTN_WIDTH), k.reshape(batch, seq, ATTN_WIDTH),
                   v.reshape(batch, seq, ATTN_WIDTH)).reshape(batch * seq, ATTN_WIDTH)
    out = _tail(o, p1, sb, h1, mod, w_ao, w_o, norm3_g[0].reshape(1, d), w_gu2, w_d2,
                final_g.reshape(1, d), seq)
    return out.reshape(batch, seq, d)
```
